```python
import math
import jax
import jax.numpy as jnp
from jax import lax
import numpy as np

D_MODEL = 2048
BATCH = 8
SEQ = 2048
DEPTH = 1

CHUNK = 64
N_META = 16
PAD_FRONT = CHUNK - N_META
HEAD_DIM = 128
GDN_V_HEADS = D_MODEL // (2 * HEAD_DIM)
GDN_QK_HEADS = GDN_V_HEADS // 2
GDN_CONV = 4
DSA_Q_HEADS = D_MODEL // (2 * HEAD_DIM)
DSA_KV_HEADS = DSA_Q_HEADS // 4
IDX_HEADS = 16
IDX_DIM = 128
TOPK_MAX = 256
ROPE_THETA = 10000.0
D_FF = 5632
FFN_CONV = 3
LN_EPS = 1e-5
RMS_EPS = 1e-6
L2_EPS = 1e-6

GDN_QK_W = GDN_QK_HEADS * HEAD_DIM
GDN_V_W = GDN_V_HEADS * HEAD_DIM
DSA_Q_W = DSA_Q_HEADS * HEAD_DIM
DSA_KV_W = DSA_KV_HEADS * HEAD_DIM
MIX_W = GDN_V_W + DSA_Q_W
SPLIT_SIZES = (GDN_QK_W, GDN_QK_W, GDN_V_W, GDN_V_W, GDN_V_HEADS, GDN_V_HEADS,
               DSA_Q_W, DSA_KV_W, DSA_KV_W, IDX_HEADS * IDX_DIM, IDX_DIM, IDX_HEADS)
IN_W = sum(SPLIT_SIZES)
SPLIT_POINTS = tuple(int(p) for p in np.cumsum(SPLIT_SIZES)[:-1])
DEEPNORM_ALPHA = (2.0 * DEPTH) ** 0.25
DEEPNORM_BETA = (8.0 * DEPTH) ** -0.25

kernel_name = "hybrid_gdn_dsa_convffn_block"


def layer_norm(x, g, b):
    xf = x.astype(jnp.float32)
    mu = jnp.mean(xf, -1, keepdims=True)
    var = jnp.mean(jnp.square(xf - mu), -1, keepdims=True)
    return ((xf - mu) * lax.rsqrt(var + LN_EPS) * g.astype(jnp.float32) + b.astype(jnp.float32)).astype(x.dtype)


def rms_norm_f32(x, g):
    xf = x.astype(jnp.float32)
    return xf * lax.rsqrt(jnp.mean(jnp.square(xf), -1, keepdims=True) + RMS_EPS) * g.astype(jnp.float32)


def l2norm(x):
    xf = x.astype(jnp.float32)
    return xf * lax.rsqrt(jnp.sum(jnp.square(xf), -1, keepdims=True) + L2_EPS)


def causal_dwconv(x, w):
    k, c = w.shape
    return lax.conv_general_dilated(x, w[:, None, :].astype(x.dtype), window_strides=(1,),
                                    padding=[(k - 1, 0)], dimension_numbers=('NWC', 'WIO', 'NWC'),
                                    feature_group_count=c)


def rope(x, pos):
    half = x.shape[-1] // 2
    inv = ROPE_THETA ** (-jnp.arange(half, dtype=jnp.float32) / half)
    ang = pos.astype(jnp.float32)[:, None] * inv[None, :]
    cos = jnp.cos(ang)[:, None, :]
    sin = jnp.sin(ang)[:, None, :]
    x1 = x[..., :half].astype(jnp.float32)
    x2 = x[..., half:].astype(jnp.float32)
    return jnp.concatenate([x1 * cos - x2 * sin, x2 * cos + x1 * sin], -1).astype(x.dtype)


def front_pad(t):
    return jnp.pad(t, ((0, 0), (PAD_FRONT, 0)) + ((0, 0),) * (t.ndim - 2))


def gated_delta_rule(q, k, v, beta, g):
    bsz, nh, tp, dk = q.shape
    dv = v.shape[-1]
    n = tp // CHUNK
    q = q.reshape(bsz, nh, n, CHUNK, dk)
    k = k.reshape(bsz, nh, n, CHUNK, dk)
    v = v.reshape(bsz, nh, n, CHUNK, dv)
    beta = beta.reshape(bsz, nh, n, CHUNK)
    G = jnp.cumsum(g.reshape(bsz, nh, n, CHUNK), -1)
    idx = jnp.arange(CHUNK)
    incl = idx[:, None] >= idx[None, :]
    strict = idx[:, None] > idx[None, :]
    decay = jnp.exp(jnp.where(incl, G[..., :, None] - G[..., None, :], -jnp.inf))
    kb = k * beta[..., None]
    lower = jnp.where(strict, jnp.einsum('bhncd,bhnsd->bhncs', kb, k) * decay, 0.0)
    rhs = jnp.concatenate([v * beta[..., None], kb * jnp.exp(G)[..., None]], -1)
    sol = lax.linalg.triangular_solve(lower + jnp.eye(CHUNK, dtype=jnp.float32), rhs,
                                      left_side=True, lower=True, unit_diagonal=True)
    u, w = sol[..., :dv], sol[..., dv:]
    intra = jnp.where(incl, jnp.einsum('bhncd,bhnsd->bhncs', q, k) * decay, 0.0)
    qg = q * jnp.exp(G)[..., None]
    kd = k * jnp.exp(G[..., -1:] - G)[..., None]
    last = jnp.exp(G[..., -1])

    def step(S, xs):
        u_n, w_n, qg_n, intra_n, kd_n, last_n = xs
        v_new = u_n - jnp.einsum('bhcd,bhde->bhce', w_n, S)
        o = jnp.einsum('bhcd,bhde->bhce', qg_n, S) + jnp.einsum('bhcs,bhse->bhce', intra_n, v_new)
        S = S * last_n[..., None, None] + jnp.einsum('bhcd,bhce->bhde', kd_n, v_new)
        return S, o

    xs = tuple(jnp.moveaxis(t, 2, 0) for t in (u, w, qg, intra, kd, last))
    s0 = jnp.zeros((bsz, nh, dk, dv), jnp.float32)
    _, o = lax.scan(step, s0, xs)
    return jnp.moveaxis(o, 0, 2).reshape(bsz, nh, tp, dv)


def gdn_group(q, k, v, z, b, a, conv_w, a_log, dt_bias, norm_g):
    bsz, t, _ = q.shape
    dtype = q.dtype
    qkv = jax.nn.silu(causal_dwconv(jnp.concatenate([q, k, v], -1), conv_w))
    q, k, v = jnp.split(qkv, [GDN_QK_W, 2 * GDN_QK_W], -1)
    rep = GDN_V_HEADS // GDN_QK_HEADS
    q = jnp.repeat(l2norm(q.reshape(bsz, t, GDN_QK_HEADS, HEAD_DIM)) * HEAD_DIM ** -0.5, rep, axis=2)
    k = jnp.repeat(l2norm(k.reshape(bsz, t, GDN_QK_HEADS, HEAD_DIM)), rep, axis=2)
    v = v.reshape(bsz, t, GDN_V_HEADS, HEAD_DIM).astype(jnp.float32)
    beta = jax.nn.sigmoid(b.astype(jnp.float32))
    g = -jnp.exp(a_log.astype(jnp.float32)) * jax.nn.softplus(a.astype(jnp.float32) + dt_bias.astype(jnp.float32))
    q, k, v, beta, g = (front_pad(x) for x in (q, k, v, beta, g))
    o = gated_delta_rule(q.transpose(0, 2, 1, 3), k.transpose(0, 2, 1, 3), v.transpose(0, 2, 1, 3),
                         beta.transpose(0, 2, 1), g.transpose(0, 2, 1))
    o = o.transpose(0, 2, 1, 3)[:, PAD_FRONT:]
    o = rms_norm_f32(o, norm_g) * jax.nn.silu(z.reshape(bsz, t, GDN_V_HEADS, HEAD_DIM).astype(jnp.float32))
    return o.reshape(bsz, t, GDN_V_W).astype(dtype)


def dsa_group(q, k, v, qi, ki, wi, ki_g, ki_b, out_g):
    bsz, t, _ = q.shape
    dtype = q.dtype
    tp = t + PAD_FRONT
    n = tp // CHUNK
    rep = DSA_Q_HEADS // DSA_KV_HEADS
    pos = jnp.arange(tp) - PAD_FRONT
    q = rope(front_pad(q).reshape(bsz, tp, DSA_Q_HEADS, HEAD_DIM), pos)
    k = rope(front_pad(k).reshape(bsz, tp, DSA_KV_HEADS, HEAD_DIM), pos)
    v = front_pad(v).reshape(bsz, tp, DSA_KV_HEADS, HEAD_DIM)
    qi = rope(front_pad(qi).reshape(bsz, tp, IDX_HEADS, IDX_DIM), pos).astype(jnp.float32)
    ki = rope(front_pad(layer_norm(ki, ki_g, ki_b))[:, :, None, :], pos)[:, :, 0].astype(jnp.float32)
    wi = front_pad(wi).astype(jnp.float32) * (IDX_HEADS ** -0.5 * IDX_DIM ** -0.5)
    n_sel = min(TOPK_MAX, SEQ // 4)
    key_pos = jnp.arange(tp)

    def block(args):
        c, q_b, qi_b, wi_b = args
        logits = jnp.einsum('bqhd,bsd->bqhs', qi_b, ki)
        score = jnp.einsum('bqhs,bqh->bqs', jax.nn.relu(logits), wi_b)
        admissible = (key_pos >= PAD_FRONT) & (key_pos < (c + 1) * CHUNK)
        score = jnp.where(admissible, score, -jnp.inf)
        _, sel = lax.top_k(score, n_sel)
        sel_ok = jnp.take(admissible, sel)
        k_sel = jax.vmap(lambda kb, ib: kb[ib])(k, sel)
        v_sel = jax.vmap(lambda vb, ib: vb[ib])(v, sel)
        qg = q_b.reshape(bsz, CHUNK, DSA_KV_HEADS, rep, HEAD_DIM)
        s = jnp.einsum('bqgrd,bqkgd->bqgrk', qg, k_sel).astype(jnp.float32) * HEAD_DIM ** -0.5
        s = jnp.where(sel_ok[:, :, None, None, :], s, -jnp.inf)
        p = jax.nn.softmax(s, -1)
        o = jnp.einsum('bqgrk,bqkgd->bqgrd', p.astype(v.dtype), v_sel)
        return o.reshape(bsz, CHUNK, DSA_Q_W)

    blocks = lambda x: jnp.moveaxis(x.reshape((bsz, n, CHUNK) + x.shape[2:]), 1, 0)
    o = lax.map(block, (jnp.arange(n), blocks(q), blocks(qi), blocks(wi)))
    o = jnp.moveaxis(o, 0, 1).reshape(bsz, tp, DSA_Q_W)[:, PAD_FRONT:]
    return rms_norm_f32(o, out_g).astype(dtype)


def conv_ffn(h, w_up, conv_w, conv_b, w_down):
    u = causal_dwconv(h @ w_up, conv_w) + conv_b
    gate, up = jnp.split(u, 2, -1)
    return (jax.nn.silu(gate) * up) @ w_down


def setup_inputs(seed: int = 0) -> dict:
    key = jax.random.key(seed)
    ks = jax.random.split(key, 20)
    f32 = jnp.float32
    nrm = lambda kk, shape, s: jax.random.normal(kk, shape, f32) * s
    dt = jnp.exp(jax.random.uniform(ks[5], (DEPTH, GDN_V_HEADS), f32, math.log(1e-3), math.log(1e-1)))
    return {
        "x": nrm(ks[0], (BATCH, SEQ, D_MODEL), 1.0),
        "meta_tokens": nrm(ks[1], (N_META, D_MODEL), 1.0),
        "w_in": nrm(ks[2], (DEPTH, D_MODEL, IN_W), D_MODEL ** -0.5),
        "gdn_conv_w": nrm(ks[3], (DEPTH, GDN_CONV, 2 * GDN_QK_W + GDN_V_W), GDN_CONV ** -0.5),
        "gdn_a_log": jnp.log(jax.random.uniform(ks[4], (DEPTH, GDN_V_HEADS), f32, 1.0, 16.0)),
        "gdn_dt_bias": dt + jnp.log(-jnp.expm1(-dt)),
        "gdn_norm_g": 1.0 + nrm(ks[6], (DEPTH, HEAD_DIM), 0.02),
        "idx_k_ln_g": 1.0 + nrm(ks[7], (DEPTH, IDX_DIM), 0.02),
        "idx_k_ln_b": nrm(ks[8], (DEPTH, IDX_DIM), 0.02),
        "attn_norm_g": 1.0 + nrm(ks[9], (DEPTH, DSA_Q_W), 0.02),
        "w_out": nrm(ks[10], (DEPTH, MIX_W, D_MODEL), MIX_W ** -0.5 * DEEPNORM_BETA),
        "ln1_g": 1.0 + nrm(ks[11], (DEPTH, D_MODEL), 0.02),
        "ln1_b": nrm(ks[12], (DEPTH, D_MODEL), 0.02),
        "w_up": nrm(ks[13], (DEPTH, D_MODEL, 2 * D_FF), D_MODEL ** -0.5),
        "ffn_conv_w": nrm(ks[14], (DEPTH, FFN_CONV, 2 * D_FF), FFN_CONV ** -0.5),
        "ffn_conv_b": nrm(ks[15], (DEPTH, 2 * D_FF), 0.01),
        "w_down": nrm(ks[16], (DEPTH, D_FF, D_MODEL), D_FF ** -0.5 * DEEPNORM_BETA),
        "ln2_g": 1.0 + nrm(ks[17], (DEPTH, D_MODEL), 0.02),
        "ln2_b": nrm(ks[18], (DEPTH, D_MODEL), 0.02),
    }


def reference(x, meta_tokens, w_in, gdn_conv_w, gdn_a_log, gdn_dt_bias, gdn_norm_g, idx_k_ln_g, idx_k_ln_b,
              attn_norm_g, w_out, ln1_g, ln1_b, w_up, ffn_conv_w, ffn_conv_b, w_down, ln2_g, ln2_b):
    bsz = x.shape[0]
    meta = jnp.broadcast_to(meta_tokens[None].astype(x.dtype), (bsz, N_META, D_MODEL))
    h = jnp.concatenate([meta, x], 1)
    for l in range(DEPTH):
        proj = h @ w_in[l]
        gq, gk, gv, gz, gb, ga, aq, ak, av, iq, ik, iw = jnp.split(proj, SPLIT_POINTS, axis=-1)
        y_a = gdn_group(gq, gk, gv, gz, gb, ga, gdn_conv_w[l], gdn_a_log[l], gdn_dt_bias[l], gdn_norm_g[l])
        y_b = dsa_group(aq, ak, av, iq, ik, iw, idx_k_ln_g[l], idx_k_ln_b[l], attn_norm_g[l])
        mix = jnp.concatenate([y_a, y_b], -1) @ w_out[l]
        h = layer_norm(DEEPNORM_ALPHA * h + mix, ln1_g[l], ln1_b[l])
        ffn = conv_ffn(h, w_up[l], ffn_conv_w[l], ffn_conv_b[l], w_down[l])
        h = layer_norm(DEEPNORM_ALPHA * h + ffn, ln2_g[l], ln2_b[l])
    return h[:, N_META:]
```

```python
import functools
import math

import jax
import jax.numpy as jnp
import numpy as np
from jax import lax
from jax.experimental import pallas as pl
from jax.experimental.pallas import tpu as pltpu

CHUNK = 64
N_META = 16
PAD_FRONT = CHUNK - N_META
HEAD_DIM = 128
GDN_V_HEADS = 8
GDN_QK_HEADS = 4
GDN_CONV = 4
DSA_Q_HEADS = 8
DSA_KV_HEADS = 2
IDX_HEADS = 16
TOPK_MAX = 256
ROPE_THETA = 10000.0
FFN_CONV = 3
LN_EPS = 1e-5
RMS_EPS = 1e-6
L2_EPS = 1e-6
DEEPNORM_ALPHA = 2.0 ** 0.25

C_GQ, C_GK, C_GV, C_GZ = 0, 512, 1024, 2048
C_AQ, C_AK, C_AV, C_IQ, C_IK, C_SM = 3072, 4096, 4352, 4608, 6656, 6784
PROJ_W = 6912
L_BETA, L_G, L_WI = 0, 8, 16

LANES = 128
VMEM_LIMIT = 56 * 1024 * 1024
HIGHEST = lax.Precision.HIGHEST
F32 = jnp.float32
BF16 = jnp.bfloat16
INT_MIN = -2 ** 31


def _dot(a, b, prec=None):
    return lax.dot_general(a, b, (((1,), (0,)), ((), ())), precision=prec,
                           preferred_element_type=F32)


def _dot_nt(a, b, prec=None):
    return lax.dot_general(a, b, (((1,), (1,)), ((), ())), precision=prec,
                           preferred_element_type=F32)


def _dot_tn(a, b, prec=None):
    return lax.dot_general(a, b, (((0,), (0,)), ((), ())), precision=prec,
                           preferred_element_type=F32)


def _params(sem):
    return pltpu.CompilerParams(dimension_semantics=sem, vmem_limit_bytes=VMEM_LIMIT)


def _silu(x):
    return x * jax.nn.sigmoid(x)


def _layer_norm(h, g, b):
    mu = jnp.mean(h, -1, keepdims=True)
    d = h - mu
    var = jnp.mean(d * d, -1, keepdims=True)
    return d * lax.rsqrt(var + LN_EPS) * g + b


def _proj_kernel(x_ref, w_ref, o_ref):
    o_ref[...] = _dot(x_ref[...], w_ref[...])


def _proj(hp, w, tm, tn):
    m, d = hp.shape
    n = w.shape[1]
    return pl.pallas_call(
        _proj_kernel,
        grid=(n // tn, m // tm),
        in_specs=[pl.BlockSpec((tm, d), lambda j, i: (i, 0)),
                  pl.BlockSpec((d, tn), lambda j, i: (0, j))],
        out_specs=pl.BlockSpec((tm, tn), lambda j, i: (i, j)),
        out_shape=jax.ShapeDtypeStruct((m, n), F32),
        compiler_params=_params(("parallel", "parallel")),
        name="in_proj",
    )(hp, w)


def _gdn_prep_kernel(x_ref, halo_ref, w_ref, o_ref, scr, *, rt, t_main):
    i = pl.program_id(1)
    j = pl.program_id(2)
    scr[0:8, :] = halo_ref[0]
    scr[8:8 + rt, :] = x_ref[0]
    w = w_ref[...]
    y = w[0:1, :] * scr[5:5 + rt, :]
    for t in range(1, GDN_CONV):
        y = y + w[t:t + 1, :] * scr[5 + t:5 + t + rt, :]
    row = i * rt + lax.broadcasted_iota(jnp.int32, (rt, 1), 0)
    is_pad = (row >= t_main) & (row < t_main + PAD_FRONT)
    y = jnp.where(is_pad, 0.0, y)
    y = _silu(y)

    @pl.when(j < 2)
    def _():
        scale = jnp.where(j == 0, HEAD_DIM ** -0.5, 1.0).astype(F32)
        for h in range(4):
            yh = y[:, h * HEAD_DIM:(h + 1) * HEAD_DIM]
            n = yh * lax.rsqrt(jnp.sum(yh * yh, -1, keepdims=True) + L2_EPS)
            o_ref[0, :, h * HEAD_DIM:(h + 1) * HEAD_DIM] = n * scale

    @pl.when(j >= 2)
    def _():
        o_ref[0] = y


def _gdn_prep(p3, conv_w, rt, t_main):
    b, f, _ = p3.shape
    nt = f // rt
    head_halo = (t_main + CHUNK) // 8 - 1

    def halo_map(bb, i, j):
        return (bb, jnp.where(i == 0, head_halo, i * (rt // 8) - 1), j)

    return pl.pallas_call(
        functools.partial(_gdn_prep_kernel, rt=rt, t_main=t_main),
        grid=(b, nt, 4),
        in_specs=[pl.BlockSpec((1, rt, 512), lambda bb, i, j: (bb, i, j)),
                  pl.BlockSpec((1, 8, 512), halo_map),
                  pl.BlockSpec((GDN_CONV, 512), lambda bb, i, j: (0, j))],
        out_specs=pl.BlockSpec((1, rt, 512), lambda bb, i, j: (bb, i, j)),
        out_shape=jax.ShapeDtypeStruct((b, f, 2048), F32),
        scratch_shapes=[pltpu.VMEM((rt + 8, 512), F32)],
        compiler_params=_params(("parallel", "parallel", "parallel")),
        name="gdn_prep",
    )(p3, p3, conv_w)


def _small_kernel(x_ref, alog_ref, dtb_ref, o_ref, *, rt, t_main):
    i = pl.program_id(1)
    x = x_ref[0]
    lane = lax.broadcasted_iota(jnp.int32, x.shape, 1)
    row = i * rt + lax.broadcasted_iota(jnp.int32, x.shape, 0)
    beta = jax.nn.sigmoid(x)
    sp_in = x + dtb_ref[...]
    softplus = jnp.maximum(sp_in, 0.0) + jnp.log1p(jnp.exp(-jnp.abs(sp_in)))
    g = -jnp.exp(alog_ref[...]) * softplus
    wi = x * (IDX_HEADS ** -0.5 * HEAD_DIM ** -0.5)
    out = jnp.where(lane < L_G, beta, jnp.where(lane < L_WI, g, wi))
    valid = (lane < L_WI + IDX_HEADS) & ~((row >= t_main) & (row < t_main + PAD_FRONT))
    o_ref[0] = jnp.where(valid, out, 0.0)


def _small(p3, alog_row, dtb_row, rt, t_main):
    b, f, _ = p3.shape
    return pl.pallas_call(
        functools.partial(_small_kernel, rt=rt, t_main=t_main),
        grid=(b, f // rt),
        in_specs=[pl.BlockSpec((1, rt, LANES), lambda bb, i: (bb, i, C_SM // LANES)),
                  pl.BlockSpec((1, LANES), lambda bb, i: (0, 0)),
                  pl.BlockSpec((1, LANES), lambda bb, i: (0, 0))],
        out_specs=pl.BlockSpec((1, rt, LANES), lambda bb, i: (bb, i, 0)),
        out_shape=jax.ShapeDtypeStruct((b, f, LANES), F32),
        compiler_params=_params(("parallel", "parallel")),
        name="gdn_small",
    )(p3, alog_row, dtb_row)


def _tri_inv(a, blk, eye):
    hd = functools.partial(_dot, prec=HIGHEST)
    ad = jnp.where(blk, a, 0.0)
    lo = jnp.where(blk, 0.0, a)
    x = eye - ad
    p = hd(ad, ad)
    x = x + hd(x, p)
    p = hd(p, p)
    x = x + hd(x, p)
    p = hd(p, p)
    x = x + hd(x, p)
    n = hd(x, lo)
    n2 = hd(n, n)
    y = eye - n
    y = y + hd(y, n2)
    return hd(y, x)


def _gdn_kernel(q_ref, k_ref, v_ref, z_ref, bg_ref, ng_ref, o_ref, s_ref):
    c = pl.program_id(1)

    @pl.when(c == 0)
    def _():
        s_ref[...] = jnp.zeros_like(s_ref)

    bg = bg_ref[0]
    ii = lax.broadcasted_iota(jnp.int32, (CHUNK, CHUNK), 0)
    jj = lax.broadcasted_iota(jnp.int32, (CHUNK, CHUNK), 1)
    incl = ii >= jj
    strict = ii > jj
    blk = (ii >> 4) == (jj >> 4)
    eye = (ii == jj).astype(F32)
    g_col = _dot(incl.astype(F32), bg, HIGHEST)
    g_row = _dot_tn(bg, (ii <= jj).astype(F32), HIGHEST)
    ng = ng_ref[...]

    qkk = []
    for a in range(GDN_QK_HEADS):
        qa = q_ref[0, :, a * HEAD_DIM:(a + 1) * HEAD_DIM]
        ka = k_ref[0, :, a * HEAD_DIM:(a + 1) * HEAD_DIM]
        qkk.append(_dot_nt(jnp.concatenate([qa, ka], 0).astype(BF16), ka.astype(BF16)))

    for h in range(GDN_V_HEADS):
        a = h // (GDN_V_HEADS // GDN_QK_HEADS)
        qa = q_ref[0, :, a * HEAD_DIM:(a + 1) * HEAD_DIM]
        ka = k_ref[0, :, a * HEAD_DIM:(a + 1) * HEAD_DIM]
        vh = v_ref[0, :, h * HEAD_DIM:(h + 1) * HEAD_DIM]
        qk = qkk[a][:CHUNK]
        kk = qkk[a][CHUNK:]
        gc = g_col[:, L_G + h:L_G + h + 1]
        gr = g_row[L_G + h:L_G + h + 1, :]
        beta = bg[:, L_BETA + h:L_BETA + h + 1]
        dm = jnp.exp(jnp.where(incl, gc - gr, -jnp.inf))
        am = jnp.where(strict, kk * beta * dm, 0.0)
        tinv = _tri_inv(am, blk, eye)
        eg = jnp.exp(gc)
        rhs = jnp.concatenate([vh * beta, ka * (beta * eg)], 1)
        uw = _dot(tinv, rhs, HIGHEST)
        u = uw[:, :HEAD_DIM]
        w = uw[:, HEAD_DIM:]
        intra = qk * dm
        qg = qa * eg
        glast = gc[CHUNK - 1:CHUNK, :]
        kd = ka * jnp.exp(glast - gc)
        s = s_ref[h]
        wq = _dot(jnp.concatenate([w, qg], 0).astype(BF16), s.astype(BF16))
        v_new = u - wq[:CHUNK]
        o = wq[CHUNK:] + _dot(intra.astype(BF16), v_new.astype(BF16))
        s_ref[h] = s * jnp.exp(glast) + _dot_tn(kd.astype(BF16), v_new.astype(BF16))
        on = o * lax.rsqrt(jnp.mean(o * o, -1, keepdims=True) + RMS_EPS) * ng
        zh = z_ref[0, :, h * HEAD_DIM:(h + 1) * HEAD_DIM]
        o_ref[0, :, h * HEAD_DIM:(h + 1) * HEAD_DIM] = (on * _silu(zh)).astype(BF16)


def _gdn(qkv, p3, bg, ng_row):
    b, f, _ = qkv.shape
    nc = f // CHUNK

    def st(c):
        return (c + nc - 1) % nc

    return pl.pallas_call(
        _gdn_kernel,
        grid=(b, nc),
        in_specs=[pl.BlockSpec((1, CHUNK, 512), lambda bb, c: (bb, st(c), 0)),
                  pl.BlockSpec((1, CHUNK, 512), lambda bb, c: (bb, st(c), 1)),
                  pl.BlockSpec((1, CHUNK, 1024), lambda bb, c: (bb, st(c), 1)),
                  pl.BlockSpec((1, CHUNK, 1024), lambda bb, c: (bb, st(c), C_GZ // 1024)),
                  pl.BlockSpec((1, CHUNK, LANES), lambda bb, c: (bb, st(c), 0)),
                  pl.BlockSpec((1, LANES), lambda bb, c: (0, 0))],
        out_specs=pl.BlockSpec((1, CHUNK, 1024), lambda bb, c: (bb, st(c), 0)),
        out_shape=jax.ShapeDtypeStruct((b, f, 1024), BF16),
        scratch_shapes=[pltpu.VMEM((GDN_V_HEADS, HEAD_DIM, HEAD_DIM), F32)],
        compiler_params=_params(("parallel", "arbitrary")),
        name="gdn_scan",
    )(qkv, qkv, qkv, p3, bg, ng_row)


def _rope(x, cos, sin_signed):
    return x * cos + pltpu.roll(x, HEAD_DIM // 2, axis=1) * sin_signed


def _rope_kernel(x_ref, cos_ref, sin_ref, o_ref):
    cos = cos_ref[...]
    sin = sin_ref[...]
    for h in range(4):
        xh = x_ref[0, :, h * HEAD_DIM:(h + 1) * HEAD_DIM]
        o_ref[0, :, h * HEAD_DIM:(h + 1) * HEAD_DIM] = _rope(xh, cos, sin).astype(o_ref.dtype)


def _rope_cols(p3, cos, sin, col0, width, rt):
    b, f, _ = p3.shape
    nb = width // 512
    off = col0 // 512
    return pl.pallas_call(
        _rope_kernel,
        grid=(b, f // rt, nb),
        in_specs=[pl.BlockSpec((1, rt, 512), lambda bb, i, j: (bb, i, off + j)),
                  pl.BlockSpec((rt, HEAD_DIM), lambda bb, i, j: (i, 0)),
                  pl.BlockSpec((rt, HEAD_DIM), lambda bb, i, j: (i, 0))],
        out_specs=pl.BlockSpec((1, rt, 512), lambda bb, i, j: (bb, i, j)),
        out_shape=jax.ShapeDtypeStruct((b, f, width), BF16),
        compiler_params=_params(("parallel", "parallel", "parallel")),
        name="dsa_q_rope",
    )(p3, cos, sin)


def _kprep_kernel(k_ref, v_ref, ik_ref, cos_ref, sin_ref, g_ref, b_ref,
                  ko_ref, vo_ref, io_ref, *, nc):
    t = pl.program_id(1)
    cos = cos_ref[...]
    sin = sin_ref[...]
    live = t < nc

    for h in range(DSA_KV_HEADS):
        kh = k_ref[0, :, h * HEAD_DIM:(h + 1) * HEAD_DIM]
        ko_ref[0, :, h * HEAD_DIM:(h + 1) * HEAD_DIM] = jnp.where(
            live, _rope(kh, cos, sin), 0.0).astype(BF16)
    vo_ref[0] = jnp.where(live, v_ref[0], 0.0).astype(BF16)
    ik = _layer_norm(ik_ref[0], g_ref[...], b_ref[...])
    row = lax.broadcasted_iota(jnp.int32, (CHUNK, 1), 0)
    keep = live & ((t > 0) | (row >= PAD_FRONT))
    io_ref[0] = jnp.where(keep, _rope(ik, cos, sin), 0.0).astype(BF16)


def _kprep(p3, cos, sin, ln_g, ln_b):
    b, f, _ = p3.shape
    nc = f // CHUNK
    nk = -(-f // LANES) * LANES // CHUNK

    def src(t):
        return jnp.where(t == 0, nc - 1, jnp.where(t >= nc, 0, t - 1))

    return pl.pallas_call(
        functools.partial(_kprep_kernel, nc=nc),
        grid=(b, nk),
        in_specs=[pl.BlockSpec((1, CHUNK, 256), lambda bb, t: (bb, src(t), C_AK // 256)),
                  pl.BlockSpec((1, CHUNK, 256), lambda bb, t: (bb, src(t), C_AV // 256)),
                  pl.BlockSpec((1, CHUNK, LANES), lambda bb, t: (bb, src(t), C_IK // LANES)),
                  pl.BlockSpec((CHUNK, HEAD_DIM), lambda bb, t: (src(t), 0)),
                  pl.BlockSpec((CHUNK, HEAD_DIM), lambda bb, t: (src(t), 0)),
                  pl.BlockSpec((1, LANES), lambda bb, t: (0, 0)),
                  pl.BlockSpec((1, LANES), lambda bb, t: (0, 0))],
        out_specs=[pl.BlockSpec((1, CHUNK, 256), lambda bb, t: (bb, t, 0)),
                   pl.BlockSpec((1, CHUNK, 256), lambda bb, t: (bb, t, 0)),
                   pl.BlockSpec((1, CHUNK, LANES), lambda bb, t: (bb, t, 0))],
        out_shape=[jax.ShapeDtypeStruct((b, nk * CHUNK, 256), BF16),
                   jax.ShapeDtypeStruct((b, nk * CHUNK, 256), BF16),
                   jax.ShapeDtypeStruct((b, nk * CHUNK, LANES), BF16)],
        compiler_params=_params(("parallel", "parallel")),
        name="dsa_k_prep",
    )(p3, p3, p3, cos, sin, ln_g, ln_b)


def _dsa_body(width, n_sel, c, q_ref, qi_ref, bg_ref, ki_ref, k_ref, v_ref, ng_ref, o_ref):
    ki = ki_ref[0, :width, :]
    bg = bg_ref[0]
    score = jnp.zeros((CHUNK, width), F32)
    for hg in range(IDX_HEADS // 4):
        lhs = jnp.concatenate(
            [qi_ref[0, :, (4 * hg + r) * HEAD_DIM:(4 * hg + r + 1) * HEAD_DIM] for r in range(4)], 0)
        lg = _dot_nt(lhs, ki)
        for r in range(4):
            h = 4 * hg + r
            score = score + jnp.maximum(lg[r * CHUNK:(r + 1) * CHUNK], 0.0) * bg[:, L_WI + h:L_WI + h + 1]
    kpos = lax.broadcasted_iota(jnp.int32, (CHUNK, width), 1)
    adm = (kpos >= PAD_FRONT) & (kpos < (c + 1) * CHUNK)
    score = jnp.where(adm, score, -jnp.inf) + 0.0
    bits = lax.bitcast_convert_type(score, jnp.int32)
    key = bits ^ ((bits >> 31) & jnp.int32(0x7FFFFFFF))

    def search():
        def it(i, lo):
            cand = lo + (jnp.int32(1) << (31 - i))
            cnt = jnp.sum((key >= cand).astype(F32), axis=1, keepdims=True)
            return jnp.where(cnt >= n_sel, cand, lo)
        return lax.fori_loop(0, 32, it, jnp.full((CHUNK, 1), INT_MIN, jnp.int32))

    n_adm = (c + 1) * CHUNK - PAD_FRONT
    thr = lax.cond(n_adm > n_sel, search, lambda: jnp.full((CHUNK, 1), INT_MIN, jnp.int32))
    sel = (key >= thr) & adm

    scale2 = HEAD_DIM ** -0.5
    rep = DSA_Q_HEADS // DSA_KV_HEADS
    outs = []
    ssq = jnp.zeros((CHUNK, 1), F32)
    for g in range(DSA_KV_HEADS):
        kg = k_ref[0, :width, g * HEAD_DIM:(g + 1) * HEAD_DIM]
        vg = v_ref[0, :width, g * HEAD_DIM:(g + 1) * HEAD_DIM]
        lhs = jnp.concatenate(
            [q_ref[0, :, (rep * g + r) * HEAD_DIM:(rep * g + r + 1) * HEAD_DIM] for r in range(rep)], 0)
        s = _dot_nt(lhs, kg)
        ps = []
        ls = []
        for r in range(rep):
            sr = jnp.where(sel, s[r * CHUNK:(r + 1) * CHUNK], -jnp.inf)
            m = jnp.max(sr, -1, keepdims=True)
            p = jnp.exp((sr - m) * scale2)
            ls.append(jnp.sum(p, -1, keepdims=True))
            ps.append(p.astype(BF16))
        o = _dot(jnp.concatenate(ps, 0), vg)
        for r in range(rep):
            orr = o[r * CHUNK:(r + 1) * CHUNK] / ls[r]
            ssq = ssq + jnp.sum(orr * orr, -1, keepdims=True)
            outs.append(orr)
    inv = lax.rsqrt(ssq / (DSA_Q_HEADS * HEAD_DIM) + RMS_EPS)
    for h in range(DSA_Q_HEADS):
        ngh = ng_ref[:, h * HEAD_DIM:(h + 1) * HEAD_DIM]
        o_ref[0, :, h * HEAD_DIM:(h + 1) * HEAD_DIM] = (outs[h] * inv * ngh).astype(BF16)


def _dsa_kernel(q_ref, qi_ref, bg_ref, ki_ref, k_ref, v_ref, ng_ref, o_ref, *, widths, n_sel):
    c = pl.program_id(1)
    need = (c + 1) * CHUNK
    lo = 0
    for wd in widths:
        @pl.when((need > lo) & (need <= wd))
        def _(wd=wd):
            _dsa_body(wd, n_sel, c, q_ref, qi_ref, bg_ref, ki_ref, k_ref, v_ref, ng_ref, o_ref)
        lo = wd


def _dsa(qr, qir, bg, kir, kr, vv, ng_row, n_sel):
    b, f, _ = qr.shape
    nc = f // CHUNK
    fk = kir.shape[1]
    widths = sorted({min(fk, -(-(fk * t) // (4 * LANES)) * LANES) for t in range(1, 5)})

    def st(c):
        return (c + nc - 1) % nc

    return pl.pallas_call(
        functools.partial(_dsa_kernel, widths=tuple(widths), n_sel=n_sel),
        grid=(b, nc),
        in_specs=[pl.BlockSpec((1, CHUNK, 1024), lambda bb, c: (bb, st(c), 0)),
                  pl.BlockSpec((1, CHUNK, 2048), lambda bb, c: (bb, st(c), 0)),
                  pl.BlockSpec((1, CHUNK, LANES), lambda bb, c: (bb, st(c), 0)),
                  pl.BlockSpec((1, fk, LANES), lambda bb, c: (bb, 0, 0)),
                  pl.BlockSpec((1, fk, 256), lambda bb, c: (bb, 0, 0)),
                  pl.BlockSpec((1, fk, 256), lambda bb, c: (bb, 0, 0)),
                  pl.BlockSpec((1, 1024), lambda bb, c: (0, 0))],
        out_specs=pl.BlockSpec((1, CHUNK, 1024), lambda bb, c: (bb, st(c), 0)),
        out_shape=jax.ShapeDtypeStruct((b, f, 1024), BF16),
        compiler_params=_params(("parallel", "arbitrary")),
        name="dsa_attn",
    )(qr, qir, bg, kir, kr, vv, ng_row)


def _outproj_kernel(ya_ref, yb_ref, res_ref, w_ref, g_ref, b_ref, o32_ref, o16_ref):
    half = ya_ref.shape[-1]
    mix = _dot(ya_ref[0], w_ref[0:half, :]) + _dot(yb_ref[0], w_ref[half:, :])
    h = DEEPNORM_ALPHA * res_ref[0] + mix
    h1 = _layer_norm(h, g_ref[...], b_ref[...])
    o32_ref[0] = h1
    o16_ref[0] = h1.astype(BF16)


def _outproj(ya, yb, res, w, g, bvec, tm, nt, row_block0, res_batched):
    b = ya.shape[0]
    d = w.shape[1]
    half = ya.shape[-1]
    res_map = (lambda bb, i: (bb, i, 0)) if res_batched else (lambda bb, i: (0, 0, 0))
    return pl.pallas_call(
        _outproj_kernel,
        grid=(b, nt),
        in_specs=[pl.BlockSpec((1, tm, half), lambda bb, i: (bb, row_block0 + i, 0)),
                  pl.BlockSpec((1, tm, half), lambda bb, i: (bb, row_block0 + i, 0)),
                  pl.BlockSpec((1, tm, d), res_map),
                  pl.BlockSpec((2 * half, d), lambda bb, i: (0, 0)),
                  pl.BlockSpec((1, d), lambda bb, i: (0, 0)),
                  pl.BlockSpec((1, d), lambda bb, i: (0, 0))],
        out_specs=[pl.BlockSpec((1, tm, d), lambda bb, i: (bb, i, 0)),
                   pl.BlockSpec((1, tm, d), lambda bb, i: (bb, i, 0))],
        out_shape=[jax.ShapeDtypeStruct((b, nt * tm, d), F32),
                   jax.ShapeDtypeStruct((b, nt * tm, d), BF16)],
        compiler_params=_params(("parallel", "parallel")),
        name="out_proj_ln1",
    )(ya, yb, res, w, g, bvec)


def _ffn_up_kernel(h_ref, hm_ref, wg_ref, wu_ref, cwg_ref, cwu_ref, cbg_ref, cbu_ref,
                   o_ref, sg, su, *, tm, tiles_per_batch):
    i = pl.program_id(1)
    first = (i % tiles_per_batch) == 0

    def conv(scr, w_ref, cw_ref, cb_ref):
        @pl.when(first)
        def _():
            scr[0:8, :] = _dot(hm_ref[...], w_ref[...])[8:16]

        @pl.when(jnp.logical_not(first))
        def _():
            scr[0:8, :] = scr[tm:tm + 8, :]

        scr[8:8 + tm, :] = _dot(h_ref[...], w_ref[...])
        cw = cw_ref[...]
        y = cw[0:1, :] * scr[6:6 + tm, :]
        for t in range(1, FFN_CONV):
            y = y + cw[t:t + 1, :] * scr[6 + t:6 + t + tm, :]
        return y + cb_ref[...]

    gate = conv(sg, wg_ref, cwg_ref, cbg_ref)
    up = conv(su, wu_ref, cwu_ref, cbu_ref)
    o_ref[...] = (_silu(gate) * up).astype(BF16)


def _ffn_up(h1b, h1m, w_up, conv_w, conv_b, tm, tn, tiles_per_batch):
    m, d = h1b.shape
    dff = w_up.shape[1] // 2
    nj = dff // tn
    return pl.pallas_call(
        functools.partial(_ffn_up_kernel, tm=tm, tiles_per_batch=tiles_per_batch),
        grid=(nj, m // tm),
        in_specs=[pl.BlockSpec((tm, d), lambda j, i: (i, 0)),
                  pl.BlockSpec((N_META, d), lambda j, i: (i // tiles_per_batch, 0)),
                  pl.BlockSpec((d, tn), lambda j, i: (0, j)),
                  pl.BlockSpec((d, tn), lambda j, i: (0, nj + j)),
                  pl.BlockSpec((FFN_CONV, tn), lambda j, i: (0, j)),
                  pl.BlockSpec((FFN_CONV, tn), lambda j, i: (0, nj + j)),
                  pl.BlockSpec((1, tn), lambda j, i: (0, j)),
                  pl.BlockSpec((1, tn), lambda j, i: (0, nj + j))],
        out_specs=pl.BlockSpec((tm, tn), lambda j, i: (i, j)),
        out_shape=jax.ShapeDtypeStruct((m, dff), BF16),
        scratch_shapes=[pltpu.VMEM((tm + 8, tn), F32), pltpu.VMEM((tm + 8, tn), F32)],
        compiler_params=_params(("parallel", "arbitrary")),
        name="ffn_up",
    )(h1b, h1m, w_up, w_up, conv_w, conv_w, conv_b, conv_b)


def _ffn_down_kernel(a_ref, w_ref, res_ref, g_ref, b_ref, o_ref, acc):
    k = pl.program_id(1)

    @pl.when(k == 0)
    def _():
        acc[...] = jnp.zeros_like(acc)

    acc[...] += _dot(a_ref[...], w_ref[...])

    @pl.when(k == pl.num_programs(1) - 1)
    def _():
        h = DEEPNORM_ALPHA * res_ref[...] + acc[...]
        o_ref[...] = _layer_norm(h, g_ref[...], b_ref[...])


def _ffn_down(act, w_down, h1, g, bvec, tm, tk):
    m, dff = act.shape
    d = w_down.shape[1]
    return pl.pallas_call(
        _ffn_down_kernel,
        grid=(m // tm, dff // tk),
        in_specs=[pl.BlockSpec((tm, tk), lambda i, k: (i, k)),
                  pl.BlockSpec((tk, d), lambda i, k: (k, 0)),
                  pl.BlockSpec((tm, d), lambda i, k: (i, 0)),
                  pl.BlockSpec((1, d), lambda i, k: (0, 0)),
                  pl.BlockSpec((1, d), lambda i, k: (0, 0))],
        out_specs=pl.BlockSpec((tm, d), lambda i, k: (i, 0)),
        out_shape=jax.ShapeDtypeStruct((m, d), F32),
        scratch_shapes=[pltpu.VMEM((tm, d), F32)],
        compiler_params=_params(("parallel", "arbitrary")),
        name="ffn_down_ln2",
    )(act, w_down, h1, g, bvec)


def _rope_tables(t_main):
    f = t_main + CHUNK
    pos = np.concatenate([np.arange(t_main) + N_META, np.zeros(PAD_FRONT), np.arange(N_META)])
    half = HEAD_DIM // 2
    inv = jnp.asarray(ROPE_THETA, F32) ** (-jnp.arange(half, dtype=F32) / half)
    ang = jnp.asarray(pos, F32)[:, None] * inv[None, :]
    cos = jnp.cos(ang)
    sin = jnp.sin(ang)
    assert ang.shape == (f, half)
    return jnp.concatenate([cos, cos], -1), jnp.concatenate([-sin, sin], -1)


def _lane_row(vec, lane0):
    return jnp.zeros((1, LANES), F32).at[0, lane0:lane0 + vec.shape[0]].set(vec.astype(F32))


def _block(x, meta_tokens, w_in, gdn_conv_w, gdn_a_log, gdn_dt_bias, gdn_norm_g, idx_k_ln_g,
           idx_k_ln_b, attn_norm_g, w_out, ln1_g, ln1_b, w_up, ffn_conv_w, ffn_conv_b, w_down,
           ln2_g, ln2_b, *, n_sel):
    b, t_main, d = x.shape
    f = t_main + CHUNK
    rt = f // 3
    tm = 512
    dff = w_down.shape[0]

    sizes = (512, 512, 1024, 1024, 8, 8, 1024, 256, 256, 2048, 128, 16)
    gq, gk, gv, gz, gb, ga, aq, ak, av, iq, ik, iw = jnp.split(w_in, np.cumsum(sizes)[:-1], axis=1)
    w_all = jnp.concatenate(
        [gq, gk, gv, gz, aq, ak, av, iq, ik, gb, ga, iw,
         jnp.zeros((d, PROJ_W - C_SM - 32), w_in.dtype)], axis=1).astype(BF16)
    cos, sin = _rope_tables(t_main)

    hp = jnp.concatenate(
        [x, jnp.zeros((b, PAD_FRONT, d), x.dtype),
         jnp.broadcast_to(meta_tokens[None], (b, N_META, d))], axis=1).astype(BF16)

    p = _proj(hp.reshape(b * f, d), w_all, rt, 768)
    p3 = p.reshape(b, f, PROJ_W)

    qkv = _gdn_prep(p3, gdn_conv_w.astype(F32), rt, t_main)
    bg = _small(p3, _lane_row(gdn_a_log, L_G), _lane_row(gdn_dt_bias, L_G), rt, t_main)
    ya = _gdn(qkv, p3, bg, gdn_norm_g.reshape(1, HEAD_DIM).astype(F32))

    qr = _rope_cols(p3, cos, sin, C_AQ, 1024, rt)
    qir = _rope_cols(p3, cos, sin, C_IQ, 2048, rt)
    kr, vv, kir = _kprep(p3, cos, sin, idx_k_ln_g.reshape(1, LANES).astype(F32),
                         idx_k_ln_b.reshape(1, LANES).astype(F32))
    yb = _dsa(qr, qir, bg, kir, kr, vv, attn_norm_g.reshape(1, -1).astype(F32), n_sel)

    w_out_b = w_out.astype(BF16)
    g1 = ln1_g.reshape(1, d).astype(F32)
    b1 = ln1_b.reshape(1, d).astype(F32)
    h1, h1b = _outproj(ya, yb, x, w_out_b, g1, b1, tm, t_main // tm, 0, True)
    _, h1m = _outproj(ya, yb, meta_tokens.reshape(1, N_META, d).astype(F32), w_out_b, g1, b1,
                      N_META, 1, (t_main + PAD_FRONT) // N_META, False)

    act = _ffn_up(h1b.reshape(b * t_main, d), h1m.reshape(b * N_META, d), w_up.astype(BF16),
                  ffn_conv_w.astype(F32), ffn_conv_b.reshape(1, -1).astype(F32),
                  tm, 512, t_main // tm)
    out = _ffn_down(act, w_down.astype(BF16), h1.reshape(b * t_main, d),
                    ln2_g.reshape(1, d).astype(F32), ln2_b.reshape(1, d).astype(F32), tm, 512)
    return out.reshape(b, t_main, d)


def kernel(x, meta_tokens, w_in, gdn_conv_w, gdn_a_log, gdn_dt_bias, gdn_norm_g, idx_k_ln_g, idx_k_ln_b,
           attn_norm_g, w_out, ln1_g, ln1_b, w_up, ffn_conv_w, ffn_conv_b, w_down, ln2_g, ln2_b):
    n_sel = min(TOPK_MAX, x.shape[1] // 4)
    return _block(x, meta_tokens, w_in[0], gdn_conv_w[0], gdn_a_log[0], gdn_dt_bias[0], gdn_norm_g[0],
                  idx_k_ln_g[0], idx_k_ln_b[0], attn_norm_g[0], w_out[0], ln1_g[0], ln1_b[0], w_up[0],
                  ffn_conv_w[0], ffn_conv_b[0], w_down[0], ln2_g[0], ln2_b[0], n_sel=n_sel)
```

```python
import functools
import math

import jax
import jax.numpy as jnp
import numpy as np
from jax import lax
from jax.experimental import pallas as pl
from jax.experimental.pallas import tpu as pltpu

CHUNK = 64
N_META = 16
PAD_FRONT = CHUNK - N_META
HEAD_DIM = 128
GDN_V_HEADS = 8
GDN_QK_HEADS = 4
GDN_CONV = 4
DSA_Q_HEADS = 8
DSA_KV_HEADS = 2
IDX_HEADS = 16
TOPK_MAX = 256
ROPE_THETA = 10000.0
FFN_CONV = 3
FFN_SUB = 256
LN_EPS = 1e-5
RMS_EPS = 1e-6
L2_EPS = 1e-6
DEEPNORM_ALPHA = 2.0 ** 0.25

C_GQ, C_GK, C_GV, C_GZ = 0, 512, 1024, 2048
C_AQ, C_AK, C_AV, C_IQ, C_IK, C_SM = 3072, 4096, 4352, 4608, 6656, 6784
PROJ_W = 6912
L_BETA, L_G, L_WI = 0, 8, 16

LANES = 128
VMEM_LIMIT = 56 * 1024 * 1024
HIGHEST = lax.Precision.HIGHEST
F32 = jnp.float32
BF16 = jnp.bfloat16
INT_MIN = -2 ** 31


def _dot(a, b, prec=None):
    return lax.dot_general(a, b, (((1,), (0,)), ((), ())), precision=prec,
                           preferred_element_type=F32)


def _dot_nt(a, b, prec=None):
    return lax.dot_general(a, b, (((1,), (1,)), ((), ())), precision=prec,
                           preferred_element_type=F32)


def _dot_tn(a, b, prec=None):
    return lax.dot_general(a, b, (((0,), (0,)), ((), ())), precision=prec,
                           preferred_element_type=F32)


def _params(sem):
    return pltpu.CompilerParams(dimension_semantics=sem, vmem_limit_bytes=VMEM_LIMIT)


def _silu(x):
    return x * jax.nn.sigmoid(x)


def _layer_norm(h, g, b):
    mu = jnp.mean(h, -1, keepdims=True)
    d = h - mu
    var = jnp.mean(d * d, -1, keepdims=True)
    return d * lax.rsqrt(var + LN_EPS) * g + b


def _proj_kernel(x_ref, w_ref, o_ref):
    o_ref[...] = _dot(x_ref[...], w_ref[...])


def _proj(hp, w, tm, tn):
    m, d = hp.shape
    n = w.shape[1]
    return pl.pallas_call(
        _proj_kernel,
        grid=(n // tn, m // tm),
        in_specs=[pl.BlockSpec((tm, d), lambda j, i: (i, 0)),
                  pl.BlockSpec((d, tn), lambda j, i: (0, j))],
        out_specs=pl.BlockSpec((tm, tn), lambda j, i: (i, j)),
        out_shape=jax.ShapeDtypeStruct((m, n), F32),
        compiler_params=_params(("parallel", "parallel")),
        name="in_proj",
    )(hp, w)


def _gdn_prep_kernel(x_ref, halo_ref, w_ref, o_ref, scr, *, rt, t_main):
    i = pl.program_id(1)
    j = pl.program_id(2)
    scr[0:8, :] = halo_ref[0]
    scr[8:8 + rt, :] = x_ref[0]
    w = w_ref[...]
    y = w[0:1, :] * scr[5:5 + rt, :]
    for t in range(1, GDN_CONV):
        y = y + w[t:t + 1, :] * scr[5 + t:5 + t + rt, :]
    row = i * rt + lax.broadcasted_iota(jnp.int32, (rt, 1), 0)
    is_pad = (row >= t_main) & (row < t_main + PAD_FRONT)
    y = jnp.where(is_pad, 0.0, y)
    y = _silu(y)

    @pl.when(j < 2)
    def _():
        scale = jnp.where(j == 0, HEAD_DIM ** -0.5, 1.0).astype(F32)
        for h in range(4):
            yh = y[:, h * HEAD_DIM:(h + 1) * HEAD_DIM]
            n = yh * lax.rsqrt(jnp.sum(yh * yh, -1, keepdims=True) + L2_EPS)
            o_ref[0, :, h * HEAD_DIM:(h + 1) * HEAD_DIM] = n * scale

    @pl.when(j >= 2)
    def _():
        o_ref[0] = y


def _gdn_prep(p3, conv_w, rt, t_main):
    b, f, _ = p3.shape
    nt = f // rt
    head_halo = (t_main + CHUNK) // 8 - 1

    def halo_map(bb, i, j):
        return (bb, jnp.where(i == 0, head_halo, i * (rt // 8) - 1), j)

    return pl.pallas_call(
        functools.partial(_gdn_prep_kernel, rt=rt, t_main=t_main),
        grid=(b, nt, 4),
        in_specs=[pl.BlockSpec((1, rt, 512), lambda bb, i, j: (bb, i, j)),
                  pl.BlockSpec((1, 8, 512), halo_map),
                  pl.BlockSpec((GDN_CONV, 512), lambda bb, i, j: (0, j))],
        out_specs=pl.BlockSpec((1, rt, 512), lambda bb, i, j: (bb, i, j)),
        out_shape=jax.ShapeDtypeStruct((b, f, 2048), F32),
        scratch_shapes=[pltpu.VMEM((rt + 8, 512), F32)],
        compiler_params=_params(("parallel", "parallel", "parallel")),
        name="gdn_prep",
    )(p3, p3, conv_w)


def _small_kernel(x_ref, alog_ref, dtb_ref, o_ref, *, rt, t_main):
    i = pl.program_id(1)
    x = x_ref[0]
    lane = lax.broadcasted_iota(jnp.int32, x.shape, 1)
    row = i * rt + lax.broadcasted_iota(jnp.int32, x.shape, 0)
    beta = jax.nn.sigmoid(x)
    sp_in = x + dtb_ref[...]
    softplus = jnp.maximum(sp_in, 0.0) + jnp.log1p(jnp.exp(-jnp.abs(sp_in)))
    g = -jnp.exp(alog_ref[...]) * softplus
    wi = x * (IDX_HEADS ** -0.5 * HEAD_DIM ** -0.5)
    out = jnp.where(lane < L_G, beta, jnp.where(lane < L_WI, g, wi))
    valid = (lane < L_WI + IDX_HEADS) & ~((row >= t_main) & (row < t_main + PAD_FRONT))
    o_ref[0] = jnp.where(valid, out, 0.0)


def _small(p3, alog_row, dtb_row, rt, t_main):
    b, f, _ = p3.shape
    return pl.pallas_call(
        functools.partial(_small_kernel, rt=rt, t_main=t_main),
        grid=(b, f // rt),
        in_specs=[pl.BlockSpec((1, rt, LANES), lambda bb, i: (bb, i, C_SM // LANES)),
                  pl.BlockSpec((1, LANES), lambda bb, i: (0, 0)),
                  pl.BlockSpec((1, LANES), lambda bb, i: (0, 0))],
        out_specs=pl.BlockSpec((1, rt, LANES), lambda bb, i: (bb, i, 0)),
        out_shape=jax.ShapeDtypeStruct((b, f, LANES), F32),
        compiler_params=_params(("parallel", "parallel")),
        name="gdn_small",
    )(p3, alog_row, dtb_row)


def _split(x):
    hi = x.astype(BF16)
    return hi, (x - hi.astype(F32)).astype(BF16)


def _dot3(a, b):
    return (_dot(a[1], b[0]) + _dot(a[0], b[1])) + _dot(a[0], b[0])


def _tri_inv_apply(ams, rhss, blk, eye):
    n = len(ams)
    ad = [_split(jnp.where(blk, a, 0.0)) for a in ams]
    lo = [_split(jnp.where(blk, 0.0, a)) for a in ams]
    x = [eye - jnp.where(blk, a, 0.0) for a in ams]
    p = ad
    for _ in range(3):
        p = [_split(_dot3(p[i], p[i])) for i in range(n)]
        x = [x[i] + _dot3(_split(x[i]), p[i]) for i in range(n)]
    xs = [_split(xi) for xi in x]
    z = [_dot3(xs[i], _split(rhss[i])) for i in range(n)]
    nn = [_dot3(xs[i], lo[i]) for i in range(n)]
    ns = [_split(m) for m in nn]
    n2 = [_split(_dot3(ns[i], ns[i])) for i in range(n)]
    y = [eye - m for m in nn]
    y = [y[i] + _dot3(_split(y[i]), n2[i]) for i in range(n)]
    return [_dot3(_split(y[i]), _split(z[i])) for i in range(n)]


def _gdn_kernel(q_ref, k_ref, v_ref, z_ref, bg_ref, ng_ref, o_ref, s_ref):
    c = pl.program_id(1)

    @pl.when(c == 0)
    def _():
        s_ref[...] = jnp.zeros_like(s_ref)

    bg = bg_ref[0]
    ii = lax.broadcasted_iota(jnp.int32, (CHUNK, CHUNK), 0)
    jj = lax.broadcasted_iota(jnp.int32, (CHUNK, CHUNK), 1)
    incl = ii >= jj
    strict = ii > jj
    blk = (ii >> 4) == (jj >> 4)
    eye = (ii == jj).astype(F32)
    g_col = _dot(incl.astype(F32), bg, HIGHEST)
    g_row = _dot_tn(bg, (ii <= jj).astype(F32), HIGHEST)
    ng = ng_ref[...]

    heads = range(GDN_V_HEADS)
    rep = GDN_V_HEADS // GDN_QK_HEADS

    qs = [q_ref[0, :, a * HEAD_DIM:(a + 1) * HEAD_DIM] for a in range(GDN_QK_HEADS)]
    ks = [k_ref[0, :, a * HEAD_DIM:(a + 1) * HEAD_DIM] for a in range(GDN_QK_HEADS)]
    qkk = [_dot_nt(jnp.concatenate([qs[a], ks[a]], 0).astype(BF16), ks[a].astype(BF16))
           for a in range(GDN_QK_HEADS)]

    gc = [g_col[:, L_G + h:L_G + h + 1] for h in heads]
    beta = [bg[:, L_BETA + h:L_BETA + h + 1] for h in heads]
    dm = [jnp.exp(jnp.where(incl, gc[h] - g_row[L_G + h:L_G + h + 1, :], -jnp.inf)) for h in heads]
    am = [jnp.where(strict, qkk[h // rep][CHUNK:] * beta[h] * dm[h], 0.0) for h in heads]
    eg = [jnp.exp(gc[h]) for h in heads]
    rhs = [jnp.concatenate([v_ref[0, :, h * HEAD_DIM:(h + 1) * HEAD_DIM] * beta[h],
                            ks[h // rep] * (beta[h] * eg[h])], 1) for h in heads]
    uw = _tri_inv_apply(am, rhs, blk, eye)

    glast = [gc[h][CHUNK - 1:CHUNK, :] for h in heads]
    s = [s_ref[h] for h in heads]
    wq = [_dot(jnp.concatenate([uw[h][:, HEAD_DIM:], qs[h // rep] * eg[h]], 0).astype(BF16),
               s[h].astype(BF16)) for h in heads]
    v_new = [(uw[h][:, :HEAD_DIM] - wq[h][:CHUNK]).astype(BF16) for h in heads]
    o = [wq[h][CHUNK:] + _dot((qkk[h // rep][:CHUNK] * dm[h]).astype(BF16), v_new[h]) for h in heads]
    for h in heads:
        kd = ks[h // rep] * jnp.exp(glast[h] - gc[h])
        s_ref[h] = s[h] * jnp.exp(glast[h]) + _dot_tn(kd.astype(BF16), v_new[h])
    for h in heads:
        on = o[h] * lax.rsqrt(jnp.mean(o[h] * o[h], -1, keepdims=True) + RMS_EPS) * ng
        zh = z_ref[0, :, h * HEAD_DIM:(h + 1) * HEAD_DIM]
        o_ref[0, :, h * HEAD_DIM:(h + 1) * HEAD_DIM] = (on * _silu(zh)).astype(BF16)


def _gdn(qkv, p3, bg, ng_row):
    b, f, _ = qkv.shape
    nc = f // CHUNK

    def st(c):
        return (c + nc - 1) % nc

    return pl.pallas_call(
        _gdn_kernel,
        grid=(b, nc),
        in_specs=[pl.BlockSpec((1, CHUNK, 512), lambda bb, c: (bb, st(c), 0)),
                  pl.BlockSpec((1, CHUNK, 512), lambda bb, c: (bb, st(c), 1)),
                  pl.BlockSpec((1, CHUNK, 1024), lambda bb, c: (bb, st(c), 1)),
                  pl.BlockSpec((1, CHUNK, 1024), lambda bb, c: (bb, st(c), C_GZ // 1024)),
                  pl.BlockSpec((1, CHUNK, LANES), lambda bb, c: (bb, st(c), 0)),
                  pl.BlockSpec((1, LANES), lambda bb, c: (0, 0))],
        out_specs=pl.BlockSpec((1, CHUNK, 1024), lambda bb, c: (bb, st(c), 0)),
        out_shape=jax.ShapeDtypeStruct((b, f, 1024), BF16),
        scratch_shapes=[pltpu.VMEM((GDN_V_HEADS, HEAD_DIM, HEAD_DIM), F32)],
        compiler_params=_params(("parallel", "arbitrary")),
        name="gdn_scan",
    )(qkv, qkv, qkv, p3, bg, ng_row)


def _rope(x, cos, sin_signed):
    return x * cos + pltpu.roll(x, HEAD_DIM // 2, axis=1) * sin_signed


def _rope_kernel(x_ref, cos_ref, sin_ref, o_ref):
    cos = cos_ref[...]
    sin = sin_ref[...]
    for h in range(4):
        xh = x_ref[0, :, h * HEAD_DIM:(h + 1) * HEAD_DIM]
        o_ref[0, :, h * HEAD_DIM:(h + 1) * HEAD_DIM] = _rope(xh, cos, sin).astype(o_ref.dtype)


def _rope_cols(p3, cos, sin, col0, width, rt):
    b, f, _ = p3.shape
    nb = width // 512
    off = col0 // 512
    return pl.pallas_call(
        _rope_kernel,
        grid=(b, f // rt, nb),
        in_specs=[pl.BlockSpec((1, rt, 512), lambda bb, i, j: (bb, i, off + j)),
                  pl.BlockSpec((rt, HEAD_DIM), lambda bb, i, j: (i, 0)),
                  pl.BlockSpec((rt, HEAD_DIM), lambda bb, i, j: (i, 0))],
        out_specs=pl.BlockSpec((1, rt, 512), lambda bb, i, j: (bb, i, j)),
        out_shape=jax.ShapeDtypeStruct((b, f, width), BF16),
        compiler_params=_params(("parallel", "parallel", "parallel")),
        name="dsa_q_rope",
    )(p3, cos, sin)


def _kprep_kernel(k_ref, v_ref, ik_ref, cos_ref, sin_ref, g_ref, b_ref,
                  ko_ref, vo_ref, io_ref, *, nc):
    t = pl.program_id(1)
    cos = cos_ref[...]
    sin = sin_ref[...]
    live = t < nc

    for h in range(DSA_KV_HEADS):
        kh = k_ref[0, :, h * HEAD_DIM:(h + 1) * HEAD_DIM]
        ko_ref[0, :, h * HEAD_DIM:(h + 1) * HEAD_DIM] = jnp.where(
            live, _rope(kh, cos, sin), 0.0).astype(BF16)
    vo_ref[0] = jnp.where(live, v_ref[0], 0.0).astype(BF16)
    ik = _layer_norm(ik_ref[0], g_ref[...], b_ref[...])
    row = lax.broadcasted_iota(jnp.int32, (CHUNK, 1), 0)
    keep = live & ((t > 0) | (row >= PAD_FRONT))
    io_ref[0] = jnp.where(keep, _rope(ik, cos, sin), 0.0).astype(BF16)


def _kprep(p3, cos, sin, ln_g, ln_b):
    b, f, _ = p3.shape
    nc = f // CHUNK
    nk = -(-f // LANES) * LANES // CHUNK

    def src(t):
        return jnp.where(t == 0, nc - 1, jnp.where(t >= nc, 0, t - 1))

    return pl.pallas_call(
        functools.partial(_kprep_kernel, nc=nc),
        grid=(b, nk),
        in_specs=[pl.BlockSpec((1, CHUNK, 256), lambda bb, t: (bb, src(t), C_AK // 256)),
                  pl.BlockSpec((1, CHUNK, 256), lambda bb, t: (bb, src(t), C_AV // 256)),
                  pl.BlockSpec((1, CHUNK, LANES), lambda bb, t: (bb, src(t), C_IK // LANES)),
                  pl.BlockSpec((CHUNK, HEAD_DIM), lambda bb, t: (src(t), 0)),
                  pl.BlockSpec((CHUNK, HEAD_DIM), lambda bb, t: (src(t), 0)),
                  pl.BlockSpec((1, LANES), lambda bb, t: (0, 0)),
                  pl.BlockSpec((1, LANES), lambda bb, t: (0, 0))],
        out_specs=[pl.BlockSpec((1, CHUNK, 256), lambda bb, t: (bb, t, 0)),
                   pl.BlockSpec((1, CHUNK, 256), lambda bb, t: (bb, t, 0)),
                   pl.BlockSpec((1, CHUNK, LANES), lambda bb, t: (bb, t, 0))],
        out_shape=[jax.ShapeDtypeStruct((b, nk * CHUNK, 256), BF16),
                   jax.ShapeDtypeStruct((b, nk * CHUNK, 256), BF16),
                   jax.ShapeDtypeStruct((b, nk * CHUNK, LANES), BF16)],
        compiler_params=_params(("parallel", "parallel")),
        name="dsa_k_prep",
    )(p3, p3, p3, cos, sin, ln_g, ln_b)


def _dsa_body(width, n_sel, c, q_ref, qi_ref, bg_ref, ki_ref, k_ref, v_ref, ng_ref, o_ref):
    ki = ki_ref[0, :width, :]
    bg = bg_ref[0]
    score = jnp.zeros((CHUNK, width), F32)
    for hg in range(IDX_HEADS // 4):
        lhs = jnp.concatenate(
            [qi_ref[0, :, (4 * hg + r) * HEAD_DIM:(4 * hg + r + 1) * HEAD_DIM] for r in range(4)], 0)
        lg = _dot_nt(lhs, ki)
        for r in range(4):
            h = 4 * hg + r
            score = score + jnp.maximum(lg[r * CHUNK:(r + 1) * CHUNK], 0.0) * bg[:, L_WI + h:L_WI + h + 1]
    kpos = lax.broadcasted_iota(jnp.int32, (CHUNK, width), 1)
    adm = (kpos >= PAD_FRONT) & (kpos < (c + 1) * CHUNK)
    score = jnp.where(adm, score, -jnp.inf) + 0.0
    bits = lax.bitcast_convert_type(score, jnp.int32)
    key = bits ^ ((bits >> 31) & jnp.int32(0x7FFFFFFF))

    def search():
        def it(i, lo):
            cand = lo + (jnp.int32(1) << (31 - i))
            cnt = jnp.sum((key >= cand).astype(F32), axis=1, keepdims=True)
            return jnp.where(cnt >= n_sel, cand, lo)
        return lax.fori_loop(0, 32, it, jnp.full((CHUNK, 1), INT_MIN, jnp.int32))

    n_adm = (c + 1) * CHUNK - PAD_FRONT
    thr = lax.cond(n_adm > n_sel, search, lambda: jnp.full((CHUNK, 1), INT_MIN, jnp.int32))
    sel = (key >= thr) & adm

    scale2 = HEAD_DIM ** -0.5
    rep = DSA_Q_HEADS // DSA_KV_HEADS
    outs = []
    ssq = jnp.zeros((CHUNK, 1), F32)
    for g in range(DSA_KV_HEADS):
        kg = k_ref[0, :width, g * HEAD_DIM:(g + 1) * HEAD_DIM]
        vg = v_ref[0, :width, g * HEAD_DIM:(g + 1) * HEAD_DIM]
        lhs = jnp.concatenate(
            [q_ref[0, :, (rep * g + r) * HEAD_DIM:(rep * g + r + 1) * HEAD_DIM] for r in range(rep)], 0)
        s = _dot_nt(lhs, kg)
        ps = []
        ls = []
        for r in range(rep):
            sr = jnp.where(sel, s[r * CHUNK:(r + 1) * CHUNK], -jnp.inf)
            m = jnp.max(sr, -1, keepdims=True)
            p = jnp.exp((sr - m) * scale2)
            ls.append(jnp.sum(p, -1, keepdims=True))
            ps.append(p.astype(BF16))
        o = _dot(jnp.concatenate(ps, 0), vg)
        for r in range(rep):
            orr = o[r * CHUNK:(r + 1) * CHUNK] / ls[r]
            ssq = ssq + jnp.sum(orr * orr, -1, keepdims=True)
            outs.append(orr)
    inv = lax.rsqrt(ssq / (DSA_Q_HEADS * HEAD_DIM) + RMS_EPS)
    for h in range(DSA_Q_HEADS):
        ngh = ng_ref[:, h * HEAD_DIM:(h + 1) * HEAD_DIM]
        o_ref[0, :, h * HEAD_DIM:(h + 1) * HEAD_DIM] = (outs[h] * inv * ngh).astype(BF16)


def _dsa_kernel(q_ref, qi_ref, bg_ref, ki_ref, k_ref, v_ref, ng_ref, o_ref, *, widths, n_sel):
    c = pl.program_id(1)
    need = (c + 1) * CHUNK
    lo = 0
    for wd in widths:
        @pl.when((need > lo) & (need <= wd))
        def _(wd=wd):
            _dsa_body(wd, n_sel, c, q_ref, qi_ref, bg_ref, ki_ref, k_ref, v_ref, ng_ref, o_ref)
        lo = wd


def _dsa(qr, qir, bg, kir, kr, vv, ng_row, n_sel):
    b, f, _ = qr.shape
    nc = f // CHUNK
    fk = kir.shape[1]
    widths = sorted({min(fk, -(-(fk * t) // (4 * LANES)) * LANES) for t in range(1, 5)})

    def st(c):
        return (c + nc - 1) % nc

    return pl.pallas_call(
        functools.partial(_dsa_kernel, widths=tuple(widths), n_sel=n_sel),
        grid=(b, nc),
        in_specs=[pl.BlockSpec((1, CHUNK, 1024), lambda bb, c: (bb, st(c), 0)),
                  pl.BlockSpec((1, CHUNK, 2048), lambda bb, c: (bb, st(c), 0)),
                  pl.BlockSpec((1, CHUNK, LANES), lambda bb, c: (bb, st(c), 0)),
                  pl.BlockSpec((1, fk, LANES), lambda bb, c: (bb, 0, 0)),
                  pl.BlockSpec((1, fk, 256), lambda bb, c: (bb, 0, 0)),
                  pl.BlockSpec((1, fk, 256), lambda bb, c: (bb, 0, 0)),
                  pl.BlockSpec((1, 1024), lambda bb, c: (0, 0))],
        out_specs=pl.BlockSpec((1, CHUNK, 1024), lambda bb, c: (bb, st(c), 0)),
        out_shape=jax.ShapeDtypeStruct((b, f, 1024), BF16),
        compiler_params=_params(("parallel", "arbitrary")),
        name="dsa_attn",
    )(qr, qir, bg, kir, kr, vv, ng_row)


def _outproj_kernel(ya_ref, yb_ref, res_ref, w_ref, g_ref, b_ref, o32_ref, o16_ref):
    half = ya_ref.shape[-1]
    mix = _dot(ya_ref[0], w_ref[0:half, :]) + _dot(yb_ref[0], w_ref[half:, :])
    h = DEEPNORM_ALPHA * res_ref[0] + mix
    h1 = _layer_norm(h, g_ref[...], b_ref[...])
    o32_ref[0] = h1
    o16_ref[0] = h1.astype(BF16)


def _outproj(ya, yb, res, w, g, bvec, tm, nt, row_block0, res_batched):
    b = ya.shape[0]
    d = w.shape[1]
    half = ya.shape[-1]
    res_map = (lambda bb, i: (bb, i, 0)) if res_batched else (lambda bb, i: (0, 0, 0))
    return pl.pallas_call(
        _outproj_kernel,
        grid=(b, nt),
        in_specs=[pl.BlockSpec((1, tm, half), lambda bb, i: (bb, row_block0 + i, 0)),
                  pl.BlockSpec((1, tm, half), lambda bb, i: (bb, row_block0 + i, 0)),
                  pl.BlockSpec((1, tm, d), res_map),
                  pl.BlockSpec((2 * half, d), lambda bb, i: (0, 0)),
                  pl.BlockSpec((1, d), lambda bb, i: (0, 0)),
                  pl.BlockSpec((1, d), lambda bb, i: (0, 0))],
        out_specs=[pl.BlockSpec((1, tm, d), lambda bb, i: (bb, i, 0)),
                   pl.BlockSpec((1, tm, d), lambda bb, i: (bb, i, 0))],
        out_shape=[jax.ShapeDtypeStruct((b, nt * tm, d), F32),
                   jax.ShapeDtypeStruct((b, nt * tm, d), BF16)],
        compiler_params=_params(("parallel", "parallel")),
        name="out_proj_ln1",
    )(ya, yb, res, w, g, bvec)


def _ffn_up_kernel(h_ref, hm_ref, wg_ref, wu_ref, cwg_ref, cwu_ref, cbg_ref, cbu_ref,
                   o_ref, sg, su, *, tm, tiles_per_batch):
    i = pl.program_id(1)
    first = (i % tiles_per_batch) == 0

    for scr, w_ref in ((sg, wg_ref), (su, wu_ref)):
        @pl.when(first)
        def _(scr=scr, w_ref=w_ref):
            scr[0:8, :] = _dot(hm_ref[...], w_ref[...])[8:16]

        @pl.when(jnp.logical_not(first))
        def _(scr=scr):
            scr[0:8, :] = scr[tm:tm + 8, :]

    def conv(scr, cw_ref, cb_ref, r0):
        cw = cw_ref[...]
        y = cw[0:1, :] * scr[6 + r0:6 + r0 + FFN_SUB, :]
        for t in range(1, FFN_CONV):
            y = y + cw[t:t + 1, :] * scr[6 + t + r0:6 + t + r0 + FFN_SUB, :]
        return y + cb_ref[...]

    for r0 in range(0, tm, FFN_SUB):
        hs = h_ref[r0:r0 + FFN_SUB, :]
        sg[8 + r0:8 + r0 + FFN_SUB, :] = _dot(hs, wg_ref[...])
        su[8 + r0:8 + r0 + FFN_SUB, :] = _dot(hs, wu_ref[...])
        gate = conv(sg, cwg_ref, cbg_ref, r0)
        up = conv(su, cwu_ref, cbu_ref, r0)
        o_ref[r0:r0 + FFN_SUB, :] = (_silu(gate) * up).astype(BF16)


def _ffn_up(h1b, h1m, w_up, conv_w, conv_b, tm, tn, tiles_per_batch):
    m, d = h1b.shape
    dff = w_up.shape[1] // 2
    nj = dff // tn
    return pl.pallas_call(
        functools.partial(_ffn_up_kernel, tm=tm, tiles_per_batch=tiles_per_batch),
        grid=(nj, m // tm),
        in_specs=[pl.BlockSpec((tm, d), lambda j, i: (i, 0)),
                  pl.BlockSpec((N_META, d), lambda j, i: (i // tiles_per_batch, 0)),
                  pl.BlockSpec((d, tn), lambda j, i: (0, j)),
                  pl.BlockSpec((d, tn), lambda j, i: (0, nj + j)),
                  pl.BlockSpec((FFN_CONV, tn), lambda j, i: (0, j)),
                  pl.BlockSpec((FFN_CONV, tn), lambda j, i: (0, nj + j)),
                  pl.BlockSpec((1, tn), lambda j, i: (0, j)),
                  pl.BlockSpec((1, tn), lambda j, i: (0, nj + j))],
        out_specs=pl.BlockSpec((tm, tn), lambda j, i: (i, j)),
        out_shape=jax.ShapeDtypeStruct((m, dff), BF16),
        scratch_shapes=[pltpu.VMEM((tm + 8, tn), F32), pltpu.VMEM((tm + 8, tn), F32)],
        compiler_params=_params(("parallel", "arbitrary")),
        name="ffn_up",
    )(h1b, h1m, w_up, w_up, conv_w, conv_w, conv_b, conv_b)


def _ffn_down_kernel(a_ref, w_ref, res_ref, g_ref, b_ref, o_ref, acc):
    k = pl.program_id(1)

    @pl.when(k == 0)
    def _():
        acc[...] = jnp.zeros_like(acc)

    acc[...] += _dot(a_ref[...], w_ref[...])

    @pl.when(k == pl.num_programs(1) - 1)
    def _():
        h = DEEPNORM_ALPHA * res_ref[...] + acc[...]
        o_ref[...] = _layer_norm(h, g_ref[...], b_ref[...])


def _ffn_down(act, w_down, h1, g, bvec, tm, tk):
    m, dff = act.shape
    d = w_down.shape[1]
    return pl.pallas_call(
        _ffn_down_kernel,
        grid=(m // tm, dff // tk),
        in_specs=[pl.BlockSpec((tm, tk), lambda i, k: (i, k)),
                  pl.BlockSpec((tk, d), lambda i, k: (k, 0)),
                  pl.BlockSpec((tm, d), lambda i, k: (i, 0)),
                  pl.BlockSpec((1, d), lambda i, k: (0, 0)),
                  pl.BlockSpec((1, d), lambda i, k: (0, 0))],
        out_specs=pl.BlockSpec((tm, d), lambda i, k: (i, 0)),
        out_shape=jax.ShapeDtypeStruct((m, d), F32),
        scratch_shapes=[pltpu.VMEM((tm, d), F32)],
        compiler_params=_params(("parallel", "arbitrary")),
        name="ffn_down_ln2",
    )(act, w_down, h1, g, bvec)


def _rope_tables(t_main):
    f = t_main + CHUNK
    pos = np.concatenate([np.arange(t_main) + N_META, np.zeros(PAD_FRONT), np.arange(N_META)])
    half = HEAD_DIM // 2
    inv = jnp.asarray(ROPE_THETA, F32) ** (-jnp.arange(half, dtype=F32) / half)
    ang = jnp.asarray(pos, F32)[:, None] * inv[None, :]
    cos = jnp.cos(ang)
    sin = jnp.sin(ang)
    assert ang.shape == (f, half)
    return jnp.concatenate([cos, cos], -1), jnp.concatenate([-sin, sin], -1)


def _lane_row(vec, lane0):
    return jnp.zeros((1, LANES), F32).at[0, lane0:lane0 + vec.shape[0]].set(vec.astype(F32))


def _block(x, meta_tokens, w_in, gdn_conv_w, gdn_a_log, gdn_dt_bias, gdn_norm_g, idx_k_ln_g,
           idx_k_ln_b, attn_norm_g, w_out, ln1_g, ln1_b, w_up, ffn_conv_w, ffn_conv_b, w_down,
           ln2_g, ln2_b, *, n_sel):
    b, t_main, d = x.shape
    f = t_main + CHUNK
    rt = f // 3
    tm = 512
    tm_up = min(1024, t_main)
    dff = w_down.shape[0]

    sizes = (512, 512, 1024, 1024, 8, 8, 1024, 256, 256, 2048, 128, 16)
    gq, gk, gv, gz, gb, ga, aq, ak, av, iq, ik, iw = jnp.split(w_in, np.cumsum(sizes)[:-1], axis=1)
    w_all = jnp.concatenate(
        [gq, gk, gv, gz, aq, ak, av, iq, ik, gb, ga, iw,
         jnp.zeros((d, PROJ_W - C_SM - 32), w_in.dtype)], axis=1).astype(BF16)
    cos, sin = _rope_tables(t_main)

    hp = jnp.concatenate(
        [x, jnp.zeros((b, PAD_FRONT, d), x.dtype),
         jnp.broadcast_to(meta_tokens[None], (b, N_META, d))], axis=1).astype(BF16)

    p = _proj(hp.reshape(b * f, d), w_all, rt, 768)
    p3 = p.reshape(b, f, PROJ_W)

    qkv = _gdn_prep(p3, gdn_conv_w.astype(F32), rt, t_main)
    bg = _small(p3, _lane_row(gdn_a_log, L_G), _lane_row(gdn_dt_bias, L_G), rt, t_main)
    ya = _gdn(qkv, p3, bg, gdn_norm_g.reshape(1, HEAD_DIM).astype(F32))

    qr = _rope_cols(p3, cos, sin, C_AQ, 1024, rt)
    qir = _rope_cols(p3, cos, sin, C_IQ, 2048, rt)
    kr, vv, kir = _kprep(p3, cos, sin, idx_k_ln_g.reshape(1, LANES).astype(F32),
                         idx_k_ln_b.reshape(1, LANES).astype(F32))
    yb = _dsa(qr, qir, bg, kir, kr, vv, attn_norm_g.reshape(1, -1).astype(F32), n_sel)

    w_out_b = w_out.astype(BF16)
    g1 = ln1_g.reshape(1, d).astype(F32)
    b1 = ln1_b.reshape(1, d).astype(F32)
    h1, h1b = _outproj(ya, yb, x, w_out_b, g1, b1, tm, t_main // tm, 0, True)
    _, h1m = _outproj(ya, yb, meta_tokens.reshape(1, N_META, d).astype(F32), w_out_b, g1, b1,
                      N_META, 1, (t_main + PAD_FRONT) // N_META, False)

    act = _ffn_up(h1b.reshape(b * t_main, d), h1m.reshape(b * N_META, d), w_up.astype(BF16),
                  ffn_conv_w.astype(F32), ffn_conv_b.reshape(1, -1).astype(F32),
                  tm_up, 512, t_main // tm_up)
    out = _ffn_down(act, w_down.astype(BF16), h1.reshape(b * t_main, d),
                    ln2_g.reshape(1, d).astype(F32), ln2_b.reshape(1, d).astype(F32), tm, dff // 4)
    return out.reshape(b, t_main, d)


def kernel(x, meta_tokens, w_in, gdn_conv_w, gdn_a_log, gdn_dt_bias, gdn_norm_g, idx_k_ln_g, idx_k_ln_b,
           attn_norm_g, w_out, ln1_g, ln1_b, w_up, ffn_conv_w, ffn_conv_b, w_down, ln2_g, ln2_b):
    n_sel = min(TOPK_MAX, x.shape[1] // 4)
    return _block(x, meta_tokens, w_in[0], gdn_conv_w[0], gdn_a_log[0], gdn_dt_bias[0], gdn_norm_g[0],
                  idx_k_ln_g[0], idx_k_ln_b[0], attn_norm_g[0], w_out[0], ln1_g[0], ln1_b[0], w_up[0],
                  ffn_conv_w[0], ffn_conv_b[0], w_down[0], ln2_g[0], ln2_b[0], n_sel=n_sel)
```

```python
import functools
import math

import jax
import jax.numpy as jnp
import numpy as np
from jax import lax
from jax.experimental import pallas as pl
from jax.experimental.pallas import tpu as pltpu

CHUNK = 64
N_META = 16
PAD_FRONT = CHUNK - N_META
HEAD_DIM = 128
GDN_V_HEADS = 8
GDN_QK_HEADS = 4
GDN_CONV = 4
DSA_Q_HEADS = 8
DSA_KV_HEADS = 2
IDX_HEADS = 16
TOPK_MAX = 256
ROPE_THETA = 10000.0
FFN_CONV = 3
FFN_SUB = 256
LN_EPS = 1e-5
RMS_EPS = 1e-6
L2_EPS = 1e-6
DEEPNORM_ALPHA = 2.0 ** 0.25

C_GQ, C_GK, C_GV, C_GZ = 0, 512, 1024, 2048
C_AQ, C_IQ, C_AK, C_AV, C_IK, C_SM = 3072, 4096, 6144, 6400, 6656, 6784
DSA_QUERIES = 128
GDN_BATCH = 2
PROJ_W = 6912
L_BETA, L_G, L_WI = 0, 8, 16

LANES = 128
VMEM_LIMIT = 56 * 1024 * 1024
HIGHEST = lax.Precision.HIGHEST
F32 = jnp.float32
BF16 = jnp.bfloat16
INT_MIN = -2 ** 31


def _dot(a, b, prec=None):
    return lax.dot_general(a, b, (((1,), (0,)), ((), ())), precision=prec,
                           preferred_element_type=F32)


def _dot_nt(a, b, prec=None):
    return lax.dot_general(a, b, (((1,), (1,)), ((), ())), precision=prec,
                           preferred_element_type=F32)


def _dot_tn(a, b, prec=None):
    return lax.dot_general(a, b, (((0,), (0,)), ((), ())), precision=prec,
                           preferred_element_type=F32)


def _params(sem):
    return pltpu.CompilerParams(dimension_semantics=sem, vmem_limit_bytes=VMEM_LIMIT)


def _silu(x):
    return x * jax.nn.sigmoid(x)


def _layer_norm(h, g, b):
    mu = jnp.mean(h, -1, keepdims=True)
    d = h - mu
    var = jnp.mean(d * d, -1, keepdims=True)
    return d * lax.rsqrt(var + LN_EPS) * g + b


def _proj_kernel(x_ref, w_ref, o_ref):
    o_ref[...] = _dot(x_ref[...], w_ref[...])


def _proj(hp, w, tm, tn):
    m, d = hp.shape
    n = w.shape[1]
    return pl.pallas_call(
        _proj_kernel,
        grid=(n // tn, m // tm),
        in_specs=[pl.BlockSpec((tm, d), lambda j, i: (i, 0)),
                  pl.BlockSpec((d, tn), lambda j, i: (0, j))],
        out_specs=pl.BlockSpec((tm, tn), lambda j, i: (i, j)),
        out_shape=jax.ShapeDtypeStruct((m, n), F32),
        compiler_params=_params(("parallel", "parallel")),
        name="in_proj",
    )(hp, w)


def _split(x):
    hi = x.astype(BF16)
    return hi, (x - hi.astype(F32)).astype(BF16)


def _dot3(a, b):
    return (_dot(a[1], b[0]) + _dot(a[0], b[1])) + _dot(a[0], b[0])


def _bdot(a, b):
    return _dot(a.astype(BF16), b.astype(BF16))


def _tri_inv_apply(ams, rhss, ii, jj, eye):
    n = len(ams)

    def same_block(log2_size):
        return (ii >> log2_size) == (jj >> log2_size)

    ad = [jnp.where(same_block(3), a, 0.0) for a in ams]
    x = [eye - m for m in ad]
    p = [_bdot(m, m) for m in ad]
    x = [x[i] + _bdot(x[i], p[i]) for i in range(n)]
    p = [_bdot(m, m) for m in p]
    x = [x[i] + _bdot(x[i], p[i]) for i in range(n)]
    for s in (3, 4, 5):
        off = same_block(s + 1) & jnp.logical_not(same_block(s))
        w = [_bdot(x[i], jnp.where(off, ams[i], 0.0)) for i in range(n)]
        x = [x[i] - _bdot(w[i], x[i]) for i in range(n)]
    ms = [_split(eye + a) for a in ams]
    for _ in range(2):
        r = [eye - _dot3(ms[i], _split(x[i])) for i in range(n)]
        x = [x[i] + _bdot(x[i], r[i]) for i in range(n)]
    return [_dot3(_split(x[i]), _split(rhss[i])) for i in range(n)]


def _gdn_kernel(qk_ref, v_ref, hqk_ref, hv_ref, z_ref, sm_ref, cw_ref, alog_ref, dtb_ref, ng_ref,
                o_ref, s_ref, xs_ref):
    c = pl.program_id(1)
    nb = qk_ref.shape[0]

    @pl.when(c == 0)
    def _():
        s_ref[...] = jnp.zeros_like(s_ref)

    half = qk_ref.shape[-1]
    qk_w = GDN_QK_HEADS * HEAD_DIM
    rep = GDN_V_HEADS // GDN_QK_HEADS
    row = lax.broadcasted_iota(jnp.int32, (CHUNK, 1), 0)
    is_pad = (c == 0) & (row < PAD_FRONT)
    ii = lax.broadcasted_iota(jnp.int32, (CHUNK, CHUNK), 0)
    jj = lax.broadcasted_iota(jnp.int32, (CHUNK, CHUNK), 1)
    incl = ii >= jj
    strict = ii > jj
    eye = (ii == jj).astype(F32)
    cw = cw_ref[...]
    ng = ng_ref[...]

    def l2n(x):
        return x * lax.rsqrt(jnp.sum(x * x, -1, keepdims=True) + L2_EPS)

    qs, ks, vs, gc, beta, dm, am, eg = [], [], [], [], [], [], [], []
    for bi in range(nb):
        xs_ref[bi, 0:8, 0:half] = hqk_ref[bi]
        xs_ref[bi, 0:8, half:] = hv_ref[bi]
        xs_ref[bi, 8:8 + CHUNK, 0:half] = qk_ref[bi]
        xs_ref[bi, 8:8 + CHUNK, half:] = v_ref[bi]
        y = cw[0:1, :] * xs_ref[bi, 5:5 + CHUNK, :]
        for t in range(1, GDN_CONV):
            y = y + cw[t:t + 1, :] * xs_ref[bi, 5 + t:5 + t + CHUNK, :]
        y = _silu(jnp.where(is_pad, 0.0, y))
        qb = [l2n(y[:, a * HEAD_DIM:(a + 1) * HEAD_DIM]) * HEAD_DIM ** -0.5 for a in range(GDN_QK_HEADS)]
        kb = [l2n(y[:, qk_w + a * HEAD_DIM:qk_w + (a + 1) * HEAD_DIM]) for a in range(GDN_QK_HEADS)]
        qkk = [_dot_nt(jnp.concatenate([qb[a], kb[a]], 0).astype(BF16), kb[a].astype(BF16))
               for a in range(GDN_QK_HEADS)]

        sm = sm_ref[bi]
        lane = lax.broadcasted_iota(jnp.int32, sm.shape, 1)
        sp_in = sm + dtb_ref[...]
        softplus = jnp.maximum(sp_in, 0.0) + jnp.log1p(jnp.exp(-jnp.abs(sp_in)))
        bg = jnp.where(lane < L_G, jax.nn.sigmoid(sm), -jnp.exp(alog_ref[...]) * softplus)
        bg = jnp.where(is_pad | (lane >= L_WI), 0.0, bg)
        g_col = _dot(incl.astype(F32), bg, HIGHEST)
        g_row = _dot_tn(bg, (ii <= jj).astype(F32), HIGHEST)

        for h in range(GDN_V_HEADS):
            a = h // rep
            qs.append(qb[a])
            ks.append(kb[a])
            vs.append(y[:, 2 * qk_w + h * HEAD_DIM:2 * qk_w + (h + 1) * HEAD_DIM])
            gc.append(g_col[:, L_G + h:L_G + h + 1])
            beta.append(bg[:, L_BETA + h:L_BETA + h + 1])
            dm.append(jnp.exp(jnp.where(incl, gc[-1] - g_row[L_G + h:L_G + h + 1, :], -jnp.inf)))
            am.append(jnp.where(strict, qkk[a][CHUNK:] * beta[-1] * dm[-1], 0.0))
            eg.append(jnp.exp(gc[-1]))
            dm[-1] = qkk[a][:CHUNK] * dm[-1]

    heads = range(nb * GDN_V_HEADS)
    rhs = [jnp.concatenate([vs[h] * beta[h], ks[h] * (beta[h] * eg[h])], 1) for h in heads]
    uw = _tri_inv_apply(am, rhs, ii, jj, eye)

    glast = [gc[h][CHUNK - 1:CHUNK, :] for h in heads]
    s = [s_ref[h] for h in heads]
    wq = [_dot(jnp.concatenate([uw[h][:, HEAD_DIM:], qs[h] * eg[h]], 0).astype(BF16),
               s[h].astype(BF16)) for h in heads]
    v_new = [(uw[h][:, :HEAD_DIM] - wq[h][:CHUNK]).astype(BF16) for h in heads]
    o = [wq[h][CHUNK:] + _dot(dm[h].astype(BF16), v_new[h]) for h in heads]
    for h in heads:
        kd = ks[h] * jnp.exp(glast[h] - gc[h])
        s_ref[h] = s[h] * jnp.exp(glast[h]) + _dot_tn(kd.astype(BF16), v_new[h])
    for h in heads:
        bi, hh = divmod(h, GDN_V_HEADS)
        on = o[h] * lax.rsqrt(jnp.mean(o[h] * o[h], -1, keepdims=True) + RMS_EPS) * ng
        zh = z_ref[bi, :, hh * HEAD_DIM:(hh + 1) * HEAD_DIM]
        o_ref[bi, :, hh * HEAD_DIM:(hh + 1) * HEAD_DIM] = (on * _silu(zh)).astype(BF16)


def _gdn(p3, conv_w, alog_row, dtb_row, ng_row):
    b, f, _ = p3.shape
    nc = f // CHUNK
    nb = GDN_BATCH

    def st(c):
        return (c + nc - 1) % nc

    def halo(c):
        return jnp.where(c >= 2, 8 * (c - 1) - 1, jnp.where(c == 1, 8 * nc - 1, 0))

    return pl.pallas_call(
        _gdn_kernel,
        grid=(b // nb, nc),
        in_specs=[pl.BlockSpec((nb, CHUNK, 1024), lambda bb, c: (bb, st(c), 0)),
                  pl.BlockSpec((nb, CHUNK, 1024), lambda bb, c: (bb, st(c), 1)),
                  pl.BlockSpec((nb, 8, 1024), lambda bb, c: (bb, halo(c), 0)),
                  pl.BlockSpec((nb, 8, 1024), lambda bb, c: (bb, halo(c), 1)),
                  pl.BlockSpec((nb, CHUNK, 1024), lambda bb, c: (bb, st(c), C_GZ // 1024)),
                  pl.BlockSpec((nb, CHUNK, LANES), lambda bb, c: (bb, st(c), C_SM // LANES)),
                  pl.BlockSpec((GDN_CONV, 2048), lambda bb, c: (0, 0)),
                  pl.BlockSpec((1, LANES), lambda bb, c: (0, 0)),
                  pl.BlockSpec((1, LANES), lambda bb, c: (0, 0)),
                  pl.BlockSpec((1, LANES), lambda bb, c: (0, 0))],
        out_specs=pl.BlockSpec((nb, CHUNK, 1024), lambda bb, c: (bb, st(c), 0)),
        out_shape=jax.ShapeDtypeStruct((b, f, 1024), BF16),
        scratch_shapes=[pltpu.VMEM((nb * GDN_V_HEADS, HEAD_DIM, HEAD_DIM), F32),
                        pltpu.VMEM((nb, CHUNK + 8, 2048), F32)],
        compiler_params=_params(("parallel", "arbitrary")),
        name="gdn_scan",
    )(p3, p3, p3, p3, p3, p3, conv_w, alog_row, dtb_row, ng_row)


def _rope(x, cos, sin_signed):
    return x * cos + pltpu.roll(x, HEAD_DIM // 2, axis=1) * sin_signed


def _kprep_kernel(k_ref, v_ref, ik_ref, cos_ref, sin_ref, g_ref, b_ref,
                  ko_ref, vo_ref, io_ref, *, rt, t_main):
    i = pl.program_id(1)
    cos = cos_ref[...]
    sin = sin_ref[...]
    for h in range(DSA_KV_HEADS):
        kh = k_ref[0, :, h * HEAD_DIM:(h + 1) * HEAD_DIM]
        ko_ref[0, :, h * HEAD_DIM:(h + 1) * HEAD_DIM] = _rope(kh, cos, sin).astype(BF16)
    vo_ref[0] = v_ref[0].astype(BF16)
    ik = _layer_norm(ik_ref[0], g_ref[...], b_ref[...])
    row = i * rt + lax.broadcasted_iota(jnp.int32, (rt, 1), 0)
    is_pad = (row >= t_main) & (row < t_main + PAD_FRONT)
    io_ref[0] = jnp.where(is_pad, 0.0, _rope(ik, cos, sin)).astype(BF16)


def _kprep(p3, cos, sin, ln_g, ln_b, rt, t_main):
    b, f, _ = p3.shape
    return pl.pallas_call(
        functools.partial(_kprep_kernel, rt=rt, t_main=t_main),
        grid=(b, f // rt),
        in_specs=[pl.BlockSpec((1, rt, 256), lambda bb, i: (bb, i, C_AK // 256)),
                  pl.BlockSpec((1, rt, 256), lambda bb, i: (bb, i, C_AV // 256)),
                  pl.BlockSpec((1, rt, LANES), lambda bb, i: (bb, i, C_IK // LANES)),
                  pl.BlockSpec((rt, HEAD_DIM), lambda bb, i: (i, 0)),
                  pl.BlockSpec((rt, HEAD_DIM), lambda bb, i: (i, 0)),
                  pl.BlockSpec((1, LANES), lambda bb, i: (0, 0)),
                  pl.BlockSpec((1, LANES), lambda bb, i: (0, 0))],
        out_specs=[pl.BlockSpec((1, rt, 256), lambda bb, i: (bb, i, 0)),
                   pl.BlockSpec((1, rt, 256), lambda bb, i: (bb, i, 0)),
                   pl.BlockSpec((1, rt, LANES), lambda bb, i: (bb, i, 0))],
        out_shape=[jax.ShapeDtypeStruct((b, f, 256), BF16),
                   jax.ShapeDtypeStruct((b, f, 256), BF16),
                   jax.ShapeDtypeStruct((b, f, LANES), BF16)],
        compiler_params=_params(("parallel", "parallel")),
        name="dsa_k_prep",
    )(p3, p3, p3, cos, sin, ln_g, ln_b)


def _sort_key(x):
    bits = lax.bitcast_convert_type(x, jnp.int32)
    return bits ^ ((bits >> 31) & jnp.int32(0x7FFFFFFF))


def _dsa_body(wm, n_sel, m0, q_ref, qi_ref, sm_ref, cos_ref, sin_ref, ki_ref, k_ref, v_ref, ng_ref,
              o_ref, keyt_ref, *, t_main):
    nq = q_ref.shape[1]
    hb0 = t_main - CHUNK
    wcat = LANES + wm
    cos = cos_ref[...]
    sin = sin_ref[...]
    wi = sm_ref[0] * (IDX_HEADS ** -0.5 * HEAD_DIM ** -0.5)

    def cat_keys(lhs, ref, c0):
        out = _dot_nt(lhs, ref[0, hb0:hb0 + LANES, c0:c0 + HEAD_DIM])
        if wm:
            out = jnp.concatenate([out, _dot_nt(lhs, ref[0, 0:wm, c0:c0 + HEAD_DIM])], 1)
        return out

    score = jnp.zeros((nq, wcat), F32)
    for hg in range(IDX_HEADS // 4):
        lhs = jnp.concatenate(
            [_rope(qi_ref[0, :, h * HEAD_DIM:(h + 1) * HEAD_DIM], cos, sin).astype(BF16)
             for h in range(4 * hg, 4 * hg + 4)], 0)
        lg = cat_keys(lhs, ki_ref, 0)
        for r in range(4):
            h = 4 * hg + r
            score = score + jnp.maximum(lg[r * nq:(r + 1) * nq], 0.0) * wi[:, L_WI + h:L_WI + h + 1]
    kpos = lax.broadcasted_iota(jnp.int32, (nq, wcat), 1)
    row = lax.broadcasted_iota(jnp.int32, (nq, wcat), 0)
    limit = LANES + CHUNK * (m0 + (row >> 6) + 1)
    adm = (kpos >= LANES - N_META) & (kpos < limit)
    score = jnp.where(adm, score, -jnp.inf) + 0.0
    key = _sort_key(score)

    def search():
        keyt_ref[0:wcat, :] = _sort_key(jnp.transpose(score))

        def it(i, lo):
            cand = lo + (jnp.int32(1) << (31 - i))
            parts = [(keyt_ref[j:j + LANES, :] >= cand).astype(F32) for j in range(0, wcat, LANES)]
            while len(parts) > 1:
                parts = [parts[j] + parts[j + 1] for j in range(0, len(parts) - 1, 2)] + parts[len(parts) & ~1:]
            acc = parts[0]
            while acc.shape[0] > 8:
                half = acc.shape[0] // 2
                acc = acc[:half] + acc[half:]
            cnt = jnp.sum(acc, axis=0, keepdims=True)
            return jnp.where(cnt >= n_sel, cand, lo)

        thr_row = lax.fori_loop(0, 32, it, jnp.full((1, nq), INT_MIN, jnp.int32))
        eye = lax.broadcasted_iota(jnp.int32, (nq, nq), 0) == lax.broadcasted_iota(jnp.int32, (nq, nq), 1)
        hi = jnp.sum(jnp.where(eye, (thr_row >> 16).astype(F32), 0.0), axis=1, keepdims=True)
        lo = jnp.sum(jnp.where(eye, (thr_row & 0xFFFF).astype(F32), 0.0), axis=1, keepdims=True)
        return (hi.astype(jnp.int32) << 16) | lo.astype(jnp.int32)

    if wm:
        max_adm = N_META + CHUNK * (m0 + nq // CHUNK)
        thr = lax.cond(max_adm > n_sel, search, lambda: jnp.full((nq, 1), INT_MIN, jnp.int32))
        sel = (key >= thr) & adm
    else:
        sel = adm

    scale2 = HEAD_DIM ** -0.5 * math.log2(math.e)
    rep = DSA_Q_HEADS // DSA_KV_HEADS
    outs = []
    ssq = jnp.zeros((nq, 1), F32)
    for g in range(DSA_KV_HEADS):
        lhs = jnp.concatenate(
            [_rope(q_ref[0, :, h * HEAD_DIM:(h + 1) * HEAD_DIM], cos, sin).astype(BF16)
             for h in range(rep * g, rep * g + rep)], 0)
        s = cat_keys(lhs, k_ref, g * HEAD_DIM)
        ps = []
        ls = []
        for r in range(rep):
            sr = jnp.where(sel, s[r * nq:(r + 1) * nq], -jnp.inf)
            m = jnp.max(sr, -1, keepdims=True)
            p = jnp.exp2((sr - m) * scale2)
            ls.append(jnp.sum(p, -1, keepdims=True))
            ps.append(p.astype(BF16))
        pc = jnp.concatenate(ps, 0)
        o = _dot(pc[:, :LANES], v_ref[0, hb0:hb0 + LANES, g * HEAD_DIM:(g + 1) * HEAD_DIM])
        if wm:
            o = o + _dot(pc[:, LANES:], v_ref[0, 0:wm, g * HEAD_DIM:(g + 1) * HEAD_DIM])
        for r in range(rep):
            orr = o[r * nq:(r + 1) * nq] / ls[r]
            ssq = ssq + jnp.sum(orr * orr, -1, keepdims=True)
            outs.append(orr)
    inv = lax.rsqrt(ssq / (DSA_Q_HEADS * HEAD_DIM) + RMS_EPS)
    for h in range(DSA_Q_HEADS):
        ngh = ng_ref[:, h * HEAD_DIM:(h + 1) * HEAD_DIM]
        o_ref[0, :, h * HEAD_DIM:(h + 1) * HEAD_DIM] = (outs[h] * inv * ngh).astype(BF16)


def _dsa_kernel(*refs, widths, n_sel, t_main):
    nq = refs[0].shape[1]
    m0 = pl.program_id(1) * (nq // CHUNK)
    if not widths:
        _dsa_body(0, n_sel, 0, *refs, t_main=t_main)
        return
    need = (m0 + nq // CHUNK) * CHUNK
    lo = 0
    for wd in widths:
        @pl.when((need > lo) & (need <= wd))
        def _(wd=wd):
            _dsa_body(wd, n_sel, m0, *refs, t_main=t_main)
        lo = wd


def _dsa(p3, cos, sin, kir, kr, vv, ng_row, n_sel, t_main, nq, head):
    b, f, _ = p3.shape
    if head:
        steps, blk0, widths = 1, t_main // nq, ()
    else:
        steps, blk0 = t_main // nq, 0
        widths = tuple(sorted({min(t_main, -(-(t_main * t) // (4 * LANES)) * LANES) for t in range(1, 5)}))
    return pl.pallas_call(
        functools.partial(_dsa_kernel, widths=widths, n_sel=n_sel, t_main=t_main),
        grid=(b, steps),
        in_specs=[pl.BlockSpec((1, nq, 1024), lambda bb, t: (bb, blk0 + t, C_AQ // 1024)),
                  pl.BlockSpec((1, nq, 2048), lambda bb, t: (bb, blk0 + t, C_IQ // 2048)),
                  pl.BlockSpec((1, nq, LANES), lambda bb, t: (bb, blk0 + t, C_SM // LANES)),
                  pl.BlockSpec((nq, HEAD_DIM), lambda bb, t: (blk0 + t, 0)),
                  pl.BlockSpec((nq, HEAD_DIM), lambda bb, t: (blk0 + t, 0)),
                  pl.BlockSpec((1, f, LANES), lambda bb, t: (bb, 0, 0)),
                  pl.BlockSpec((1, f, 256), lambda bb, t: (bb, 0, 0)),
                  pl.BlockSpec((1, f, 256), lambda bb, t: (bb, 0, 0)),
                  pl.BlockSpec((1, 1024), lambda bb, t: (0, 0))],
        out_specs=pl.BlockSpec((1, nq, 1024), lambda bb, t: (bb, t, 0)),
        out_shape=jax.ShapeDtypeStruct((b, steps * nq, 1024), BF16),
        scratch_shapes=[pltpu.VMEM((LANES + t_main, nq), jnp.int32)],
        compiler_params=_params(("parallel", "arbitrary")),
        name="dsa_head" if head else "dsa_attn",
    )(p3, p3, p3, cos, sin, kir, kr, vv, ng_row)


def _outproj_kernel(ya_ref, yb_ref, res_ref, w_ref, g_ref, b_ref, o32_ref, o16_ref):
    half = ya_ref.shape[-1]
    mix = _dot(ya_ref[0], w_ref[0:half, :]) + _dot(yb_ref[0], w_ref[half:, :])
    h = DEEPNORM_ALPHA * res_ref[0] + mix
    h1 = _layer_norm(h, g_ref[...], b_ref[...])
    o32_ref[0] = h1
    o16_ref[0] = h1.astype(BF16)


def _outproj(ya, yb, res, w, g, bvec, tm, nt, ya_block0, yb_block0, res_batched):
    b = ya.shape[0]
    d = w.shape[1]
    half = ya.shape[-1]
    res_map = (lambda bb, i: (bb, i, 0)) if res_batched else (lambda bb, i: (0, 0, 0))
    return pl.pallas_call(
        _outproj_kernel,
        grid=(b, nt),
        in_specs=[pl.BlockSpec((1, tm, half), lambda bb, i: (bb, ya_block0 + i, 0)),
                  pl.BlockSpec((1, tm, half), lambda bb, i: (bb, yb_block0 + i, 0)),
                  pl.BlockSpec((1, tm, d), res_map),
                  pl.BlockSpec((2 * half, d), lambda bb, i: (0, 0)),
                  pl.BlockSpec((1, d), lambda bb, i: (0, 0)),
                  pl.BlockSpec((1, d), lambda bb, i: (0, 0))],
        out_specs=[pl.BlockSpec((1, tm, d), lambda bb, i: (bb, i, 0)),
                   pl.BlockSpec((1, tm, d), lambda bb, i: (bb, i, 0))],
        out_shape=[jax.ShapeDtypeStruct((b, nt * tm, d), F32),
                   jax.ShapeDtypeStruct((b, nt * tm, d), BF16)],
        compiler_params=_params(("parallel", "parallel")),
        name="out_proj_ln1",
    )(ya, yb, res, w, g, bvec)


def _ffn_up_kernel(h_ref, hm_ref, wg_ref, wu_ref, cwg_ref, cwu_ref, cbg_ref, cbu_ref,
                   o_ref, sg, su, *, tm, tiles_per_batch):
    i = pl.program_id(1)
    first = (i % tiles_per_batch) == 0

    for scr, w_ref in ((sg, wg_ref), (su, wu_ref)):
        @pl.when(first)
        def _(scr=scr, w_ref=w_ref):
            scr[0:8, :] = _dot(hm_ref[...], w_ref[...])[8:16]

        @pl.when(jnp.logical_not(first))
        def _(scr=scr):
            scr[0:8, :] = scr[tm:tm + 8, :]

    def conv(scr, cw_ref, cb_ref, r0):
        cw = cw_ref[...]
        y = cw[0:1, :] * scr[6 + r0:6 + r0 + FFN_SUB, :]
        for t in range(1, FFN_CONV):
            y = y + cw[t:t + 1, :] * scr[6 + t + r0:6 + t + r0 + FFN_SUB, :]
        return y + cb_ref[...]

    for r0 in range(0, tm, FFN_SUB):
        hs = h_ref[r0:r0 + FFN_SUB, :]
        sg[8 + r0:8 + r0 + FFN_SUB, :] = _dot(hs, wg_ref[...])
        su[8 + r0:8 + r0 + FFN_SUB, :] = _dot(hs, wu_ref[...])
        gate = conv(sg, cwg_ref, cbg_ref, r0)
        up = conv(su, cwu_ref, cbu_ref, r0)
        o_ref[r0:r0 + FFN_SUB, :] = (_silu(gate) * up).astype(BF16)


def _ffn_up(h1b, h1m, w_up, conv_w, conv_b, tm, tn, tiles_per_batch):
    m, d = h1b.shape
    dff = w_up.shape[1] // 2
    nj = dff // tn
    return pl.pallas_call(
        functools.partial(_ffn_up_kernel, tm=tm, tiles_per_batch=tiles_per_batch),
        grid=(nj, m // tm),
        in_specs=[pl.BlockSpec((tm, d), lambda j, i: (i, 0)),
                  pl.BlockSpec((N_META, d), lambda j, i: (i // tiles_per_batch, 0)),
                  pl.BlockSpec((d, tn), lambda j, i: (0, j)),
                  pl.BlockSpec((d, tn), lambda j, i: (0, nj + j)),
                  pl.BlockSpec((FFN_CONV, tn), lambda j, i: (0, j)),
                  pl.BlockSpec((FFN_CONV, tn), lambda j, i: (0, nj + j)),
                  pl.BlockSpec((1, tn), lambda j, i: (0, j)),
                  pl.BlockSpec((1, tn), lambda j, i: (0, nj + j))],
        out_specs=pl.BlockSpec((tm, tn), lambda j, i: (i, j)),
        out_shape=jax.ShapeDtypeStruct((m, dff), BF16),
        scratch_shapes=[pltpu.VMEM((tm + 8, tn), F32), pltpu.VMEM((tm + 8, tn), F32)],
        compiler_params=_params(("parallel", "arbitrary")),
        name="ffn_up",
    )(h1b, h1m, w_up, w_up, conv_w, conv_w, conv_b, conv_b)


def _ffn_down_kernel(a_ref, w_ref, res_ref, g_ref, b_ref, o_ref, acc):
    k = pl.program_id(1)

    @pl.when(k == 0)
    def _():
        acc[...] = jnp.zeros_like(acc)

    acc[...] += _dot(a_ref[...], w_ref[...])

    @pl.when(k == pl.num_programs(1) - 1)
    def _():
        h = DEEPNORM_ALPHA * res_ref[...] + acc[...]
        o_ref[...] = _layer_norm(h, g_ref[...], b_ref[...])


def _ffn_down(act, w_down, h1, g, bvec, tm, tk):
    m, dff = act.shape
    d = w_down.shape[1]
    return pl.pallas_call(
        _ffn_down_kernel,
        grid=(m // tm, dff // tk),
        in_specs=[pl.BlockSpec((tm, tk), lambda i, k: (i, k)),
                  pl.BlockSpec((tk, d), lambda i, k: (k, 0)),
                  pl.BlockSpec((tm, d), lambda i, k: (i, 0)),
                  pl.BlockSpec((1, d), lambda i, k: (0, 0)),
                  pl.BlockSpec((1, d), lambda i, k: (0, 0))],
        out_specs=pl.BlockSpec((tm, d), lambda i, k: (i, 0)),
        out_shape=jax.ShapeDtypeStruct((m, d), F32),
        scratch_shapes=[pltpu.VMEM((tm, d), F32)],
        compiler_params=_params(("parallel", "arbitrary")),
        name="ffn_down_ln2",
    )(act, w_down, h1, g, bvec)


def _rope_tables(t_main):
    f = t_main + CHUNK
    pos = np.concatenate([np.arange(t_main) + N_META, np.zeros(PAD_FRONT), np.arange(N_META)])
    half = HEAD_DIM // 2
    inv = jnp.asarray(ROPE_THETA, F32) ** (-jnp.arange(half, dtype=F32) / half)
    ang = jnp.asarray(pos, F32)[:, None] * inv[None, :]
    cos = jnp.cos(ang)
    sin = jnp.sin(ang)
    assert ang.shape == (f, half)
    return jnp.concatenate([cos, cos], -1), jnp.concatenate([-sin, sin], -1)


def _lane_row(vec, lane0):
    return jnp.zeros((1, LANES), F32).at[0, lane0:lane0 + vec.shape[0]].set(vec.astype(F32))


def _mixers(x, meta_tokens, w_in, gdn_conv_w, gdn_a_log, gdn_dt_bias, gdn_norm_g, idx_k_ln_g,
            idx_k_ln_b, attn_norm_g, n_sel):
    b, t_main, d = x.shape
    f = t_main + CHUNK
    rt = f // 3

    sizes = (512, 512, 1024, 1024, 8, 8, 1024, 256, 256, 2048, 128, 16)
    gq, gk, gv, gz, gb, ga, aq, ak, av, iq, ik, iw = jnp.split(w_in, np.cumsum(sizes)[:-1], axis=1)
    w_all = jnp.concatenate(
        [gq, gk, gv, gz, aq, iq, ak, av, ik, gb, ga, iw,
         jnp.zeros((d, PROJ_W - C_SM - 32), w_in.dtype)], axis=1).astype(BF16)
    cos, sin = _rope_tables(t_main)

    hp = jnp.concatenate(
        [x, jnp.zeros((b, PAD_FRONT, d), x.dtype),
         jnp.broadcast_to(meta_tokens[None], (b, N_META, d))], axis=1).astype(BF16)

    p = _proj(hp.reshape(b * f, d), w_all, rt, 768)
    p3 = p.reshape(b, f, PROJ_W)

    ya = _gdn(p3, gdn_conv_w.astype(F32), _lane_row(gdn_a_log, L_G), _lane_row(gdn_dt_bias, L_G),
              gdn_norm_g.reshape(1, HEAD_DIM).astype(F32))

    kr, vv, kir = _kprep(p3, cos, sin, idx_k_ln_g.reshape(1, LANES).astype(F32),
                         idx_k_ln_b.reshape(1, LANES).astype(F32), rt, t_main)
    ng = attn_norm_g.reshape(1, -1).astype(F32)
    yb = _dsa(p3, cos, sin, kir, kr, vv, ng, n_sel, t_main, DSA_QUERIES, False)
    yb_head = _dsa(p3, cos, sin, kir, kr, vv, ng, n_sel, t_main, CHUNK, True)
    return ya, yb, yb_head


def _block(x, meta_tokens, w_in, gdn_conv_w, gdn_a_log, gdn_dt_bias, gdn_norm_g, idx_k_ln_g,
           idx_k_ln_b, attn_norm_g, w_out, ln1_g, ln1_b, w_up, ffn_conv_w, ffn_conv_b, w_down,
           ln2_g, ln2_b, *, n_sel):
    b, t_main, d = x.shape
    tm = 512
    tm_up = min(1024, t_main)
    dff = w_down.shape[0]

    ya, yb, yb_head = _mixers(x, meta_tokens, w_in, gdn_conv_w, gdn_a_log, gdn_dt_bias, gdn_norm_g,
                              idx_k_ln_g, idx_k_ln_b, attn_norm_g, n_sel)

    w_out_b = w_out.astype(BF16)
    g1 = ln1_g.reshape(1, d).astype(F32)
    b1 = ln1_b.reshape(1, d).astype(F32)
    h1, h1b = _outproj(ya, yb, x, w_out_b, g1, b1, tm, t_main // tm, 0, 0, True)
    _, h1m = _outproj(ya, yb_head, meta_tokens.reshape(1, N_META, d).astype(F32), w_out_b, g1, b1,
                      N_META, 1, (t_main + PAD_FRONT) // N_META, PAD_FRONT // N_META, False)

    act = _ffn_up(h1b.reshape(b * t_main, d), h1m.reshape(b * N_META, d), w_up.astype(BF16),
                  ffn_conv_w.astype(F32), ffn_conv_b.reshape(1, -1).astype(F32),
                  tm_up, 512, t_main // tm_up)
    out = _ffn_down(act, w_down.astype(BF16), h1.reshape(b * t_main, d),
                    ln2_g.reshape(1, d).astype(F32), ln2_b.reshape(1, d).astype(F32), tm, dff // 4)
    return out.reshape(b, t_main, d)


def kernel(x, meta_tokens, w_in, gdn_conv_w, gdn_a_log, gdn_dt_bias, gdn_norm_g, idx_k_ln_g, idx_k_ln_b,
           attn_norm_g, w_out, ln1_g, ln1_b, w_up, ffn_conv_w, ffn_conv_b, w_down, ln2_g, ln2_b):
    n_sel = min(TOPK_MAX, x.shape[1] // 4)
    return _block(x, meta_tokens, w_in[0], gdn_conv_w[0], gdn_a_log[0], gdn_dt_bias[0], gdn_norm_g[0],
                  idx_k_ln_g[0], idx_k_ln_b[0], attn_norm_g[0], w_out[0], ln1_g[0], ln1_b[0], w_up[0],
                  ffn_conv_w[0], ffn_conv_b[0], w_down[0], ln2_g[0], ln2_b[0], n_sel=n_sel)
```
